```python
import math
import jax, jax.numpy as jnp
from jax import lax
import numpy as np

D_MODEL = 1024
BATCH = 16
SEQ = 2048
DEPTH = 4

HEAD_DIM = 64
N_MIXERS = 4
GROUP_WIDTH = D_MODEL // N_MIXERS
GROUP_HEADS = GROUP_WIDTH // HEAD_DIM
CHUNK = 128
POOL_WINDOWS = (2, 4, 8, 16)
POOL_GROUPS = len(POOL_WINDOWS)
POOL_GROUP_DIM = GROUP_WIDTH // POOL_GROUPS
WINDOW = 128
SWA_Q_HEADS = GROUP_HEADS
SWA_KV_HEADS = 2
SWA_GROUP = SWA_Q_HEADS // SWA_KV_HEADS
SB_HEADS = GROUP_HEADS
SB_BLOCK = 128
N_BUCKETS = 32
MAX_DISTANCE = 128
D_FF = ((8 * D_MODEL // 3 + 255) // 256) * 256
EPS = 1e-6

IN_SIZES = (GROUP_WIDTH, GROUP_WIDTH, GROUP_WIDTH,
            GROUP_WIDTH, SWA_KV_HEADS * HEAD_DIM, SWA_KV_HEADS * HEAD_DIM,
            GROUP_WIDTH, GROUP_WIDTH, GROUP_WIDTH)
D_IN = sum(IN_SIZES)

kernel_name = "hybrid_parallel_headgroup_trunk"


def _split_points():
    pts, acc = [], 0
    for s in IN_SIZES[:-1]:
        acc += s
        pts.append(acc)
    return pts


def _rmsnorm(x, g):
    xf = x.astype(jnp.float32)
    y = xf * lax.rsqrt(jnp.mean(xf * xf, axis=-1, keepdims=True) + EPS)
    return (y * g.astype(jnp.float32)).astype(x.dtype)


def _layernorm_noaffine(x):
    xf = x.astype(jnp.float32)
    mu = jnp.mean(xf, axis=-1, keepdims=True)
    xc = xf - mu
    y = xc * lax.rsqrt(jnp.mean(xc * xc, axis=-1, keepdims=True) + EPS)
    return y.astype(x.dtype)


def _chunked_sgu(u, v, w_s, b_s):
    B, S, _ = u.shape
    nc = S // CHUNK
    u = jax.nn.gelu(u)
    v = _layernorm_noaffine(jax.nn.gelu(v).reshape(B, nc, CHUNK, GROUP_HEADS, HEAD_DIM))
    causal = jnp.tril(jnp.ones((CHUNK, CHUNK), w_s.dtype))
    w = w_s * causal[None]
    mix = jnp.einsum('hts,bnshd->bnthd', w, v) + b_s.T[None, None, :, :, None]
    return u * mix.reshape(B, S, GROUP_WIDTH)


def _multiscale_pool(p, w_pool, scale):
    B, S, _ = p.shape
    pg = p.reshape(B, S, POOL_GROUPS, POOL_GROUP_DIM)
    csum = jnp.cumsum(pg.astype(jnp.float32), axis=1)
    csum = jnp.pad(csum, ((0, 0), (1, 0), (0, 0), (0, 0)))
    t = jnp.arange(S)[:, None]
    win = jnp.array(POOL_WINDOWS, jnp.int32)[None, :]
    start = jnp.maximum(t + 1 - win, 0)
    gidx = jnp.arange(POOL_GROUPS)[None, :]
    window_sum = csum[:, 1:] - csum[:, start, gidx]
    count = (t + 1 - start).astype(jnp.float32)
    pooled = window_sum / count[None, :, :, None]
    y = pooled.astype(p.dtype) - pg
    y = jnp.einsum('bsgc,gcd->bsgd', y, w_pool)
    return y.reshape(B, S, GROUP_WIDTH) * scale


def _t5_bucket(dist):
    max_exact = N_BUCKETS // 2
    df = jnp.maximum(dist, 1).astype(jnp.float32)
    large = max_exact + (jnp.log(df / max_exact) / math.log(MAX_DISTANCE / max_exact)
                         * (N_BUCKETS - max_exact)).astype(jnp.int32)
    large = jnp.minimum(large, N_BUCKETS - 1)
    return jnp.where(dist < max_exact, dist, large)


def _swa_sink_attention(q, k, v, sinks, rel_bias):
    B, S, _ = q.shape
    nb = S // WINDOW
    qb = q.reshape(B, nb, WINDOW, SWA_KV_HEADS, SWA_GROUP, HEAD_DIM)
    kb = k.reshape(B, nb, WINDOW, SWA_KV_HEADS, HEAD_DIM)
    vb = v.reshape(B, nb, WINDOW, SWA_KV_HEADS, HEAD_DIM)
    pad = ((0, 0), (1, 0), (0, 0), (0, 0), (0, 0))
    k2 = jnp.concatenate([jnp.pad(kb, pad)[:, :-1], kb], axis=2)
    v2 = jnp.concatenate([jnp.pad(vb, pad)[:, :-1], vb], axis=2)
    logits = jnp.einsum('bnqhgd,bnkhd->bnhgqk', qb, k2,
                        preferred_element_type=jnp.float32) * (HEAD_DIM ** -0.5)
    dist = (jnp.arange(WINDOW)[:, None] + WINDOW) - jnp.arange(2 * WINDOW)[None, :]
    in_window = (dist >= 0) & (dist < WINDOW)
    bias = rel_bias.astype(jnp.float32)[_t5_bucket(jnp.clip(dist, 0, WINDOW - 1))]
    bias = bias.transpose(2, 0, 1).reshape(SWA_KV_HEADS, SWA_GROUP, WINDOW, 2 * WINDOW)
    not_pad = (jnp.arange(nb)[:, None] > 0) | (jnp.arange(2 * WINDOW)[None, :] >= WINDOW)
    mask = in_window[None] & not_pad[:, None, :]
    logits = jnp.where(mask[None, :, None, None], logits + bias, -1e30)
    sink = jnp.broadcast_to(sinks.astype(jnp.float32).reshape(SWA_KV_HEADS, SWA_GROUP, 1, 1),
                            logits.shape[:-1] + (1,))
    probs = jax.nn.softmax(jnp.concatenate([logits, sink], axis=-1), axis=-1)[..., :-1]
    out = jnp.einsum('bnhgqk,bnkhd->bnqhgd', probs.astype(v.dtype), v2)
    return out.reshape(B, S, GROUP_WIDTH)


def _stick_breaking_attention(q, k, v):
    B, S, _ = q.shape
    nb = S // SB_BLOCK
    kh = k.reshape(B, S, SB_HEADS, HEAD_DIM)
    vh = v.reshape(B, S, SB_HEADS, HEAD_DIM)
    qb = q.reshape(B, nb, SB_BLOCK, SB_HEADS, HEAD_DIM).transpose(1, 0, 2, 3, 4)
    key_pos = jnp.arange(S)
    scale = HEAD_DIM ** -0.5

    def block(args):
        qblk, n = args
        z = jnp.einsum('bqhd,bkhd->bhqk', qblk, kh,
                       preferred_element_type=jnp.float32) * scale
        q_pos = n * SB_BLOCK + jnp.arange(SB_BLOCK)
        causal = key_pos[None, :] < q_pos[:, None]
        log_1m = jnp.where(causal, jax.nn.log_sigmoid(-z), 0.0)
        tail = lax.cumsum(log_1m, axis=3, reverse=True) - log_1m
        w = jnp.where(causal, jnp.exp(jax.nn.log_sigmoid(z) + tail), 0.0)
        return jnp.einsum('bhqk,bkhd->bqhd', w.astype(v.dtype), vh)

    out = lax.map(block, (qb, jnp.arange(nb)))
    return out.transpose(1, 0, 2, 3, 4).reshape(B, S, GROUP_WIDTH)


def setup_inputs(seed: int = 0) -> dict:
    key = jax.random.key(seed)
    ks = jax.random.split(key, 16)
    f32 = jnp.float32
    nrm = lambda k, shape, s: jax.random.normal(k, shape, f32) * s
    return {
        "x": jax.random.normal(ks[0], (BATCH, SEQ, D_MODEL), f32),
        "w_in": nrm(ks[1], (DEPTH, D_MODEL, D_IN), D_MODEL ** -0.5),
        "w_out": nrm(ks[2], (DEPTH, D_MODEL, D_MODEL), D_MODEL ** -0.5),
        "sgu_w": nrm(ks[3], (DEPTH, GROUP_HEADS, CHUNK, CHUNK), CHUNK ** -0.5),
        "sgu_b": 1.0 + nrm(ks[4], (DEPTH, GROUP_HEADS, CHUNK), 0.02),
        "pool_w": nrm(ks[5], (DEPTH, POOL_GROUPS, POOL_GROUP_DIM, POOL_GROUP_DIM), POOL_GROUP_DIM ** -0.5),
        "pool_scale": 1.0 + nrm(ks[6], (DEPTH, GROUP_WIDTH), 0.02),
        "swa_sinks": nrm(ks[7], (DEPTH, SWA_Q_HEADS), 1.0),
        "rel_bias": nrm(ks[8], (N_BUCKETS, SWA_Q_HEADS), 0.5),
        "mix_out_gain": 1.0 + nrm(ks[9], (DEPTH, D_MODEL), 0.02),
        "norm_mix": 1.0 + nrm(ks[10], (DEPTH, D_MODEL), 0.02),
        "norm_ffn": 1.0 + nrm(ks[11], (DEPTH, D_MODEL), 0.02),
        "w_gate_up": nrm(ks[12], (DEPTH, D_MODEL, 2 * D_FF), D_MODEL ** -0.5),
        "w_down": nrm(ks[13], (DEPTH, D_FF, D_MODEL), D_FF ** -0.5),
        "norm_final": 1.0 + nrm(ks[14], (D_MODEL,), 0.02),
    }


def reference(x, w_in, w_out, sgu_w, sgu_b, pool_w, pool_scale, swa_sinks, rel_bias,
              mix_out_gain, norm_mix, norm_ffn, w_gate_up, w_down, norm_final):
    B, S, _ = x.shape
    splits = _split_points()
    for l in range(DEPTH):
        h = _rmsnorm(x, norm_mix[l])
        proj = h @ w_in[l]
        a_u, a_v, b_in, c_q, c_k, c_v, d_q, d_k, d_v = jnp.split(proj, splits, axis=-1)
        y_a = _chunked_sgu(a_u, a_v, sgu_w[l], sgu_b[l])
        y_b = _multiscale_pool(b_in, pool_w[l], pool_scale[l])
        y_c = _swa_sink_attention(c_q, c_k, c_v, swa_sinks[l], rel_bias)
        y_d = _stick_breaking_attention(d_q, d_k, d_v)
        ycat = jnp.stack([y_a, y_b, y_c, y_d], axis=2)
        ycat = _rmsnorm(ycat, mix_out_gain[l].reshape(N_MIXERS, GROUP_WIDTH))
        x = x + ycat.reshape(B, S, D_MODEL) @ w_out[l]
        h = _rmsnorm(x, norm_ffn[l])
        gate, up = jnp.split(h @ w_gate_up[l], 2, axis=-1)
        x = x + (jax.nn.silu(gate) * up) @ w_down[l]
    return _rmsnorm(x, norm_final)
```

```python
import functools
import math

import numpy as np
import jax
import jax.numpy as jnp
from jax import lax
from jax.experimental import pallas as pl
from jax.experimental.pallas import tpu as pltpu

F32 = jnp.float32
BF16 = jnp.bfloat16

HEAD_DIM = 64
GROUP_WIDTH = 256
CHUNK = 128
POOL_WINDOWS = (2, 4, 8, 16)
POOL_HALO = 16
N_BUCKETS = 32
MAX_DISTANCE = 128
EPS = 1e-6
MASK_VALUE = -1e30

LOCAL_ROWS = 512
SB_BLOCK = 256
DENSE_ROWS = 512
FF_CHUNK = 256
VMEM_LIMIT_DENSE = 52 * 1024 * 1024
VMEM_LIMIT_MIX = 40 * 1024 * 1024


def _rms(x, g):
    return x * lax.rsqrt(jnp.mean(x * x, axis=-1, keepdims=True) + EPS) * g


def _gelu_tanh(x):
    inner = math.sqrt(2.0 / math.pi) * (x + 0.044715 * (x * x * x))
    return x * (0.5 * (1.0 + jnp.tanh(inner)))


def _split_bf16(x):
    hi = x.astype(BF16)
    lo = (x - hi.astype(F32)).astype(BF16)
    return hi, lo


def _dot(a, b):
    return jnp.dot(a, b, preferred_element_type=F32)


def _dot_nt(a, b):
    return lax.dot_general(a, b, (((1,), (1,)), ((), ())), preferred_element_type=F32)


def _norm_proj_kernel(x_ref, g_ref, w_ref, pa_ref, pc_ref, pd_ref):
    h = _rms(x_ref[...], g_ref[...]).astype(BF16)
    y = _dot(h, w_ref[...])
    na = pa_ref.shape[1]
    nc = pc_ref.shape[1]
    pa_ref[...] = y[:, :na]
    pc_ref[...] = y[:, na:na + nc].astype(BF16)
    pd_ref[...] = y[:, na + nc:].astype(BF16)


def _norm_proj(x2d, g, w, na, nc, nd):
    m, d = x2d.shape
    tm = DENSE_ROWS
    return pl.pallas_call(
        _norm_proj_kernel,
        grid=(m // tm,),
        in_specs=[
            pl.BlockSpec((tm, d), lambda i: (i, 0)),
            pl.BlockSpec((1, d), lambda i: (0, 0)),
            pl.BlockSpec(memory_space=pltpu.VMEM),
        ],
        out_specs=[
            pl.BlockSpec((tm, na), lambda i: (i, 0)),
            pl.BlockSpec((tm, nc), lambda i: (i, 0)),
            pl.BlockSpec((tm, nd), lambda i: (i, 0)),
        ],
        out_shape=[
            jax.ShapeDtypeStruct((m, na), F32),
            jax.ShapeDtypeStruct((m, nc), BF16),
            jax.ShapeDtypeStruct((m, nd), BF16),
        ],
        compiler_params=pltpu.CompilerParams(
            dimension_semantics=("arbitrary",), vmem_limit_bytes=VMEM_LIMIT_MIX),
        name="norm_proj",
    )(x2d, g, w)


def _local_mix_kernel(relb_ref, sink_ref,
                      pa_ref, halo_ref, pc_ref, kh_ref, vh_ref,
                      sguw_ref, sgub_ref, avg_ref, poolw_ref, pscale_ref,
                      bucket_ref, gain_ref,
                      o_ref,
                      bias_ref):
    b = pl.program_id(0)
    j = pl.program_id(1)
    rows = pa_ref.shape[0]
    gw = GROUP_WIDTH
    lane = lax.broadcasted_iota(jnp.int32, (1, gw), 1)
    head_of_lane = lane // HEAD_DIM

    @pl.when(jnp.logical_and(b == 0, j == 0))
    def _():
        bucket = bucket_ref[...]
        for hq in range(bias_ref.shape[0]):
            acc = jnp.zeros(bucket.shape, F32)
            for bk in range(N_BUCKETS):
                acc = jnp.where(bucket == bk, relb_ref[bk, hq], acc)
            bias_ref[hq] = acc

    gain = gain_ref[...]

    p = pa_ref[:, 2 * gw:3 * gw]
    halo = jnp.where(j == 0, 0.0, halo_ref[...])
    ext = jnp.concatenate([halo, p], axis=0)
    sums = []
    run = ext
    for shift in (1, 2, 4, 8):
        run = run + pltpu.roll(run, shift, 0)
        sums.append(run[POOL_HALO:])
    group = lane // (gw // len(POOL_WINDOWS))
    wsum = jnp.where(group == 0, sums[0],
                     jnp.where(group == 1, sums[1],
                               jnp.where(group == 2, sums[2], sums[3])))
    win = jnp.where(group == 0, POOL_WINDOWS[0],
                    jnp.where(group == 1, POOL_WINDOWS[1],
                              jnp.where(group == 2, POOL_WINDOWS[2], POOL_WINDOWS[3])))
    t_seq = j * rows + lax.broadcasted_iota(jnp.int32, (rows, 1), 0)
    count = jnp.minimum(t_seq + 1, win).astype(F32)
    yb = wsum / count - p
    yb = _dot(yb.astype(BF16), poolw_ref[...]) * pscale_ref[...]
    o_ref[:, gw:2 * gw] = _rms(yb, gain[:, gw:2 * gw]).astype(o_ref.dtype)

    tril = (lax.broadcasted_iota(jnp.int32, (CHUNK, CHUNK), 1)
            <= lax.broadcasted_iota(jnp.int32, (CHUNK, CHUNK), 0))
    qi = lax.broadcasted_iota(jnp.int32, (CHUNK, 2 * CHUNK), 0)
    ki = lax.broadcasted_iota(jnp.int32, (CHUNK, 2 * CHUNK), 1)
    in_window = jnp.logical_and(ki > qi, ki <= qi + CHUNK)
    avg = avg_ref[...]
    n_heads = gw // HEAD_DIM

    for c in range(rows // CHUNK):
        r0 = c * CHUNK
        u = _gelu_tanh(pa_ref[r0:r0 + CHUNK, 0:gw])
        v = _gelu_tanh(pa_ref[r0:r0 + CHUNK, gw:2 * gw])
        v_hi, v_lo = _split_bf16(v)
        vc = v - (_dot(v_hi, avg) + _dot(v_lo, avg))
        sq_hi, sq_lo = _split_bf16(vc * vc)
        vn = vc * lax.rsqrt(_dot(sq_hi, avg) + _dot(sq_lo, avg) + EPS)
        vn = vn.astype(BF16)
        mix = sgub_ref[...]
        for h in range(n_heads):
            w_h = jnp.where(tril, sguw_ref[h], 0.0).astype(BF16)
            mix = mix + jnp.where(head_of_lane == h, _dot(w_h, vn), 0.0)
        ya = u * mix
        o_ref[r0:r0 + CHUNK, 0:gw] = _rms(ya, gain[:, 0:gw]).astype(o_ref.dtype)

        q = pc_ref[r0:r0 + CHUNK, 0:gw]
        if c == 0:
            k_prev, v_prev = kh_ref[...], vh_ref[...]
        else:
            k_prev = pc_ref[r0 - CHUNK:r0, gw:2 * gw]
            v_prev = pc_ref[r0 - CHUNK:r0, 2 * gw:3 * gw]
        k2 = jnp.concatenate([k_prev, pc_ref[r0:r0 + CHUNK, gw:2 * gw]], axis=0)
        v2 = jnp.concatenate([v_prev, pc_ref[r0:r0 + CHUNK, 2 * gw:3 * gw]], axis=0)
        if c == 0:
            mask = jnp.logical_and(in_window, jnp.logical_or(ki >= CHUNK, j > 0))
        else:
            mask = in_window
        yc = jnp.zeros((CHUNK, gw), F32)
        for hq in range(n_heads):
            qm = jnp.where(head_of_lane == hq, q, jnp.zeros_like(q))
            logits = jnp.where(mask, _dot_nt(qm, k2) + bias_ref[hq], MASK_VALUE)
            sink = sink_ref[hq]
            mx = jnp.maximum(jnp.max(logits, axis=-1, keepdims=True), sink)
            e = jnp.exp(logits - mx)
            denom = jnp.sum(e, axis=-1, keepdims=True) + jnp.exp(sink - mx)
            probs = (e / denom).astype(BF16)
            yc = yc + jnp.where(head_of_lane == hq, _dot(probs, v2), 0.0)
        o_ref[r0:r0 + CHUNK, 2 * gw:3 * gw] = _rms(yc, gain[:, 2 * gw:3 * gw]).astype(o_ref.dtype)


def _local_mix(pa, pc, rel_bias, sinks, sgu_w, sgu_b_tile, avg, pool_bd, pool_scale,
               bucket, gain, batch, seq):
    m = pa.shape[0]
    rows = LOCAL_ROWS
    nq = seq // rows
    gw = GROUP_WIDTH
    n_heads = gw // HEAD_DIM

    def row_blk(b, j):
        return b * nq + j

    halo_per_blk = rows // POOL_HALO
    kv_per_blk = rows // CHUNK
    smem = pl.BlockSpec(memory_space=pltpu.SMEM)
    whole = pl.BlockSpec(memory_space=pltpu.VMEM)
    return pl.pallas_call(
        _local_mix_kernel,
        grid=(batch, nq),
        in_specs=[
            smem, smem,
            pl.BlockSpec((rows, 3 * gw), lambda b, j: (row_blk(b, j), 0)),
            pl.BlockSpec((POOL_HALO, gw),
                         lambda b, j: (jnp.maximum(row_blk(b, j) * halo_per_blk - 1, 0), 2)),
            pl.BlockSpec((rows, 3 * gw), lambda b, j: (row_blk(b, j), 0)),
            pl.BlockSpec((CHUNK, gw),
                         lambda b, j: (jnp.maximum(row_blk(b, j) * kv_per_blk - 1, 0), 1)),
            pl.BlockSpec((CHUNK, gw),
                         lambda b, j: (jnp.maximum(row_blk(b, j) * kv_per_blk - 1, 0), 2)),
            whole, whole, whole, whole, whole, whole, whole,
        ],
        out_specs=pl.BlockSpec((rows, 3 * gw), lambda b, j: (row_blk(b, j), 0)),
        out_shape=jax.ShapeDtypeStruct((m, 3 * gw), BF16),
        scratch_shapes=[pltpu.VMEM((n_heads, CHUNK, 2 * CHUNK), F32)],
        compiler_params=pltpu.CompilerParams(
            dimension_semantics=("arbitrary", "arbitrary"), vmem_limit_bytes=VMEM_LIMIT_MIX),
        name="local_mix",
    )(rel_bias, sinks, pa, pa, pc, pc, pc, sgu_w, sgu_b_tile, avg, pool_bd, pool_scale,
      bucket, gain)


def _softplus(z):
    return jnp.maximum(z, 0.0) + jnp.log(1.0 + jnp.exp(-jnp.abs(z)))


def _sb_attn_kernel(q_ref, k_ref, v_ref, upper_ref, gain_ref, o_ref):
    j = pl.program_id(1)
    blk = q_ref.shape[0]
    gw = q_ref.shape[1]
    half = gw // 2
    n_heads = gw // HEAD_DIM
    lane = lax.broadcasted_iota(jnp.int32, (1, gw), 1)
    head_of_lane = lane // HEAD_DIM
    q = q_ref[...]
    upper = upper_ref[...]
    row = lax.broadcasted_iota(jnp.int32, (blk, blk), 0)
    col = lax.broadcasted_iota(jnp.int32, (blk, blk), 1)
    causal = col < row

    def tile(qm, start, carry, diag, vcol):
        kk = k_ref[pl.ds(start, blk), :]
        vv = v_ref[pl.ds(start, blk), vcol:vcol + half]
        z = _dot_nt(qm, kk)
        sp = _softplus(z)
        if diag:
            sp = jnp.where(causal, sp, 0.0)
        hi, lo = _split_bf16(sp)
        tail = _dot(hi, upper) + _dot(lo, upper)
        if carry is not None:
            tail = tail + carry
        w = jnp.exp(z - sp - tail)
        if diag:
            w = jnp.where(causal, w, 0.0)
        return _dot(w.astype(BF16), vv), jnp.sum(sp, axis=-1, keepdims=True)

    accs = []
    for h in range(n_heads):
        qm = jnp.where(head_of_lane == h, q, jnp.zeros_like(q))
        vcol = (h // 2) * half
        acc, carry = tile(qm, pl.multiple_of(j * blk, blk), None, True, vcol)

        def body(i, state, qm=qm, vcol=vcol):
            acc, carry = state
            start = pl.multiple_of((j - 1 - i) * blk, blk)
            pv, rs = tile(qm, start, carry, False, vcol)
            return acc + pv, carry + rs

        acc, _ = lax.fori_loop(0, j, body, (acc, carry))
        accs.append(acc)

    lane_h = lax.broadcasted_iota(jnp.int32, (1, half), 1)
    y = jnp.concatenate(
        [jnp.where(lane_h < HEAD_DIM, accs[0], accs[1]),
         jnp.where(lane_h < HEAD_DIM, accs[2], accs[3])], axis=-1)
    o_ref[...] = _rms(y, gain_ref[...]).astype(o_ref.dtype)


def _sb_attn(pd, upper, gain, batch, seq):
    m = pd.shape[0]
    gw = GROUP_WIDTH
    blk = SB_BLOCK
    nq = seq // blk
    whole = pl.BlockSpec(memory_space=pltpu.VMEM)
    return pl.pallas_call(
        _sb_attn_kernel,
        grid=(batch, nq),
        in_specs=[
            pl.BlockSpec((blk, gw), lambda b, j: (b * nq + j, 0)),
            pl.BlockSpec((seq, gw), lambda b, j: (b, 1)),
            pl.BlockSpec((seq, gw), lambda b, j: (b, 2)),
            whole, whole,
        ],
        out_specs=pl.BlockSpec((blk, gw), lambda b, j: (b * nq + j, 0)),
        out_shape=jax.ShapeDtypeStruct((m, gw), BF16),
        compiler_params=pltpu.CompilerParams(
            dimension_semantics=("arbitrary", "arbitrary"), vmem_limit_bytes=VMEM_LIMIT_MIX),
        name="sb_attn",
    )(pd, pd, pd, upper, gain)


def _out_ffn_kernel(x_ref, yabc_ref, yd_ref, wout_ref, g_ref, wgu_ref, wdown_ref, gfin_ref,
                    o_ref, x2_ref, act_ref, *, final_norm):
    n_abc = yabc_ref.shape[1]
    d_ff = wdown_ref.shape[0]
    x2_ref[...] = (x_ref[...]
                   + _dot(yabc_ref[...], wout_ref[0:n_abc, :])
                   + _dot(yd_ref[...], wout_ref[n_abc:, :]))
    h = _rms(x2_ref[...], g_ref[...]).astype(BF16)
    for c in range(d_ff // FF_CHUNK):
        c0 = c * FF_CHUNK
        gate = _dot(h, wgu_ref[:, c0:c0 + FF_CHUNK])
        up = _dot(h, wgu_ref[:, d_ff + c0:d_ff + c0 + FF_CHUNK])
        act_ref[:, c0:c0 + FF_CHUNK] = (gate * (1.0 / (1.0 + jnp.exp(-gate))) * up).astype(BF16)
    out = x2_ref[...] + _dot(act_ref[...], wdown_ref[...])
    if final_norm:
        out = _rms(out, gfin_ref[...])
    o_ref[...] = out


def _out_ffn(x2d, yabc, yd, w_out, g, w_gu, w_down, g_final, final_norm):
    m, d = x2d.shape
    tm = DENSE_ROWS
    whole = pl.BlockSpec(memory_space=pltpu.VMEM)
    vec = pl.BlockSpec((1, d), lambda i: (0, 0))
    return pl.pallas_call(
        functools.partial(_out_ffn_kernel, final_norm=final_norm),
        grid=(m // tm,),
        in_specs=[
            pl.BlockSpec((tm, d), lambda i: (i, 0)),
            pl.BlockSpec((tm, yabc.shape[1]), lambda i: (i, 0)),
            pl.BlockSpec((tm, yd.shape[1]), lambda i: (i, 0)),
            whole, vec, whole, whole, vec,
        ],
        out_specs=pl.BlockSpec((tm, d), lambda i: (i, 0)),
        out_shape=jax.ShapeDtypeStruct((m, d), F32),
        scratch_shapes=[pltpu.VMEM((tm, d), F32), pltpu.VMEM((tm, w_down.shape[0]), BF16)],
        compiler_params=pltpu.CompilerParams(
            dimension_semantics=("arbitrary",), vmem_limit_bytes=VMEM_LIMIT_DENSE),
        name="out_ffn",
    )(x2d, yabc, yd, w_out, g, w_gu, w_down, g_final)


def _t5_bucket(dist):
    max_exact = N_BUCKETS // 2
    df = jnp.maximum(dist, 1).astype(F32)
    large = max_exact + (jnp.log(df / max_exact) / math.log(MAX_DISTANCE / max_exact)
                         * (N_BUCKETS - max_exact)).astype(jnp.int32)
    large = jnp.minimum(large, N_BUCKETS - 1)
    return jnp.where(dist < max_exact, dist, large)


def kernel(x, w_in, w_out, sgu_w, sgu_b, pool_w, pool_scale, swa_sinks, rel_bias, mix_out_gain,
           norm_mix, norm_ffn, w_gate_up, w_down, norm_final):
    batch, seq, d = x.shape
    depth = w_in.shape[0]
    gw = GROUP_WIDTH
    n_heads = gw // HEAD_DIM
    kv_heads = (w_in.shape[2] - 7 * gw) // (2 * HEAD_DIM)
    rep = n_heads // kv_heads
    assert seq % LOCAL_ROWS == 0 and seq % SB_BLOCK == 0 and (batch * seq) % DENSE_ROWS == 0

    head_id = np.arange(gw) // HEAD_DIM
    avg = jnp.asarray((head_id[:, None] == head_id[None, :]) / HEAD_DIM, BF16)
    upper = jnp.asarray(np.arange(SB_BLOCK)[:, None] > np.arange(SB_BLOCK)[None, :], BF16)
    dist = (jnp.arange(CHUNK)[:, None] + CHUNK) - jnp.arange(2 * CHUNK)[None, :]
    bucket = _t5_bucket(jnp.clip(dist, 0, CHUNK - 1)).astype(jnp.int32)
    pool_group = gw // len(POOL_WINDOWS)
    scale = HEAD_DIM ** -0.5

    off = np.cumsum([0, gw, gw, gw, gw, kv_heads * HEAD_DIM, kv_heads * HEAD_DIM, gw, gw])
    expand = np.repeat(np.arange(kv_heads), rep)[:, None] * HEAD_DIM + np.arange(HEAD_DIM)[None, :]
    expand = expand.reshape(-1)
    cols = np.concatenate([
        np.arange(0, 3 * gw),
        off[3] + np.arange(gw), off[4] + expand, off[5] + expand,
        off[6] + np.arange(3 * gw)])
    col_scale = np.ones(cols.shape[0], np.float32)
    col_scale[3 * gw:4 * gw] = scale
    col_scale[6 * gw:7 * gw] = scale

    x2d = x.reshape(batch * seq, d)
    for l in range(depth):
        w_l = (w_in[l][:, cols] * col_scale).astype(BF16)
        pa, pc, pd = _norm_proj(x2d, norm_mix[l].reshape(1, d), w_l, 3 * gw, 3 * gw, 3 * gw)

        pool_bd = jnp.zeros((gw, gw), F32)
        for g in range(len(POOL_WINDOWS)):
            pool_bd = lax.dynamic_update_slice(pool_bd, pool_w[l, g], (g * pool_group, g * pool_group))
        sgu_b_tile = jnp.repeat(sgu_b[l].T, HEAD_DIM, axis=1)
        gain = mix_out_gain[l].reshape(1, 4 * gw)
        yabc = _local_mix(pa, pc, rel_bias, swa_sinks[l], sgu_w[l], sgu_b_tile, avg,
                          pool_bd.astype(BF16), pool_scale[l].reshape(1, gw), bucket,
                          gain[:, :3 * gw], batch, seq)
        yd = _sb_attn(pd, upper, gain[:, 3 * gw:], batch, seq)

        x2d = _out_ffn(x2d, yabc, yd, w_out[l].astype(BF16), norm_ffn[l].reshape(1, d),
                       w_gate_up[l].astype(BF16), w_down[l].astype(BF16),
                       norm_final.reshape(1, d), l == depth - 1)
    return x2d.reshape(batch, seq, d)
```

```python
import functools
import math

import numpy as np
import jax
import jax.numpy as jnp
from jax import lax
from jax.experimental import pallas as pl
from jax.experimental.pallas import tpu as pltpu

F32 = jnp.float32
BF16 = jnp.bfloat16

HEAD_DIM = 64
GROUP_WIDTH = 256
CHUNK = 128
POOL_WINDOWS = (2, 4, 8, 16)
POOL_HALO = 16
N_BUCKETS = 32
MAX_DISTANCE = 128
EPS = 1e-6
MASK_VALUE = -1e30

LOCAL_ROWS = 512
SB_BLOCK = 256
DENSE_ROWS = 512
FF_CHUNK = 256
VMEM_LIMIT_DENSE = 52 * 1024 * 1024
VMEM_LIMIT_MIX = 40 * 1024 * 1024


def _rms(x, g):
    return x * lax.rsqrt(jnp.mean(x * x, axis=-1, keepdims=True) + EPS) * g


def _gelu_tanh(x):
    inner = math.sqrt(2.0 / math.pi) * (x + 0.044715 * (x * x * x))
    return x * (0.5 * (1.0 + jnp.tanh(inner)))


def _split_bf16(x):
    hi = x.astype(BF16)
    lo = (x - hi.astype(F32)).astype(BF16)
    return hi, lo


def _dot(a, b):
    return jnp.dot(a, b, preferred_element_type=F32)


def _dot_nt(a, b):
    return lax.dot_general(a, b, (((1,), (1,)), ((), ())), preferred_element_type=F32)


def _norm_proj_kernel(x_ref, g_ref, w_ref, pa_ref, pc_ref, pd_ref):
    h = _rms(x_ref[...], g_ref[...]).astype(BF16)
    y = _dot(h, w_ref[...])
    na = pa_ref.shape[1]
    nc = pc_ref.shape[1]
    pa_ref[...] = y[:, :na]
    pc_ref[...] = y[:, na:na + nc].astype(BF16)
    pd_ref[...] = y[:, na + nc:].astype(BF16)


def _norm_proj(x2d, g, w, na, nc, nd):
    m, d = x2d.shape
    tm = DENSE_ROWS
    return pl.pallas_call(
        _norm_proj_kernel,
        grid=(m // tm,),
        in_specs=[
            pl.BlockSpec((tm, d), lambda i: (i, 0)),
            pl.BlockSpec((1, d), lambda i: (0, 0)),
            pl.BlockSpec(memory_space=pltpu.VMEM),
        ],
        out_specs=[
            pl.BlockSpec((tm, na), lambda i: (i, 0)),
            pl.BlockSpec((tm, nc), lambda i: (i, 0)),
            pl.BlockSpec((tm, nd), lambda i: (i, 0)),
        ],
        out_shape=[
            jax.ShapeDtypeStruct((m, na), F32),
            jax.ShapeDtypeStruct((m, nc), BF16),
            jax.ShapeDtypeStruct((m, nd), BF16),
        ],
        compiler_params=pltpu.CompilerParams(
            dimension_semantics=("arbitrary",), vmem_limit_bytes=VMEM_LIMIT_MIX),
        name="norm_proj",
    )(x2d, g, w)


def _local_mix_kernel(relb_ref, sink_ref,
                      pa_ref, halo_ref, pc_ref, kh_ref, vh_ref,
                      sguw_ref, sgub_ref, avg_ref, poolw_ref, pscale_ref,
                      bucket_ref, gain_ref,
                      o_ref,
                      bias_ref):
    b = pl.program_id(0)
    j = pl.program_id(1)
    rows = pa_ref.shape[0]
    gw = GROUP_WIDTH
    lane = lax.broadcasted_iota(jnp.int32, (1, gw), 1)
    head_of_lane = lane // HEAD_DIM

    @pl.when(jnp.logical_and(b == 0, j == 0))
    def _():
        bucket = bucket_ref[...]
        for hq in range(bias_ref.shape[0]):
            acc = jnp.zeros(bucket.shape, F32)
            for bk in range(N_BUCKETS):
                acc = jnp.where(bucket == bk, relb_ref[bk, hq], acc)
            bias_ref[hq] = acc

    gain = gain_ref[...]

    p = pa_ref[:, 2 * gw:3 * gw]
    halo = jnp.where(j == 0, 0.0, halo_ref[...])
    ext = jnp.concatenate([halo, p], axis=0)
    sums = []
    run = ext
    for shift in (1, 2, 4, 8):
        run = run + pltpu.roll(run, shift, 0)
        sums.append(run[POOL_HALO:])
    group = lane // (gw // len(POOL_WINDOWS))
    wsum = jnp.where(group == 0, sums[0],
                     jnp.where(group == 1, sums[1],
                               jnp.where(group == 2, sums[2], sums[3])))
    win = jnp.where(group == 0, POOL_WINDOWS[0],
                    jnp.where(group == 1, POOL_WINDOWS[1],
                              jnp.where(group == 2, POOL_WINDOWS[2], POOL_WINDOWS[3])))
    t_seq = j * rows + lax.broadcasted_iota(jnp.int32, (rows, 1), 0)
    count = jnp.minimum(t_seq + 1, win).astype(F32)
    yb = wsum / count - p
    yb = _dot(yb.astype(BF16), poolw_ref[...]) * pscale_ref[...]
    o_ref[:, gw:2 * gw] = _rms(yb, gain[:, gw:2 * gw]).astype(o_ref.dtype)

    tril = (lax.broadcasted_iota(jnp.int32, (CHUNK, CHUNK), 1)
            <= lax.broadcasted_iota(jnp.int32, (CHUNK, CHUNK), 0))
    qi = lax.broadcasted_iota(jnp.int32, (CHUNK, 2 * CHUNK), 0)
    ki = lax.broadcasted_iota(jnp.int32, (CHUNK, 2 * CHUNK), 1)
    in_window = jnp.logical_and(ki > qi, ki <= qi + CHUNK)
    avg = avg_ref[...]
    n_heads = gw // HEAD_DIM

    for c in range(rows // CHUNK):
        r0 = c * CHUNK
        u = _gelu_tanh(pa_ref[r0:r0 + CHUNK, 0:gw])
        v = _gelu_tanh(pa_ref[r0:r0 + CHUNK, gw:2 * gw])
        v_hi, v_lo = _split_bf16(v)
        vc = v - (_dot(v_hi, avg) + _dot(v_lo, avg))
        sq_hi, sq_lo = _split_bf16(vc * vc)
        vn = vc * lax.rsqrt(_dot(sq_hi, avg) + _dot(sq_lo, avg) + EPS)
        vn = vn.astype(BF16)
        mix = sgub_ref[...]
        for h in range(n_heads):
            w_h = jnp.where(tril, sguw_ref[h], 0.0).astype(BF16)
            mix = mix + jnp.where(head_of_lane == h, _dot(w_h, vn), 0.0)
        ya = u * mix
        o_ref[r0:r0 + CHUNK, 0:gw] = _rms(ya, gain[:, 0:gw]).astype(o_ref.dtype)

        q = pc_ref[r0:r0 + CHUNK, 0:gw]
        if c == 0:
            k_prev, v_prev = kh_ref[...], vh_ref[...]
        else:
            k_prev = pc_ref[r0 - CHUNK:r0, gw:2 * gw]
            v_prev = pc_ref[r0 - CHUNK:r0, 2 * gw:3 * gw]
        k2 = jnp.concatenate([k_prev, pc_ref[r0:r0 + CHUNK, gw:2 * gw]], axis=0)
        v2 = jnp.concatenate([v_prev, pc_ref[r0:r0 + CHUNK, 2 * gw:3 * gw]], axis=0)
        if c == 0:
            mask = jnp.logical_and(in_window, jnp.logical_or(ki >= CHUNK, j > 0))
        else:
            mask = in_window
        yc = jnp.zeros((CHUNK, gw), F32)
        for hq in range(n_heads):
            qm = jnp.where(head_of_lane == hq, q, jnp.zeros_like(q))
            logits = jnp.where(mask, _dot_nt(qm, k2) + bias_ref[hq], MASK_VALUE)
            sink = sink_ref[hq]
            mx = jnp.maximum(jnp.max(logits, axis=-1, keepdims=True), sink)
            e = jnp.exp(logits - mx)
            denom = jnp.sum(e, axis=-1, keepdims=True) + jnp.exp(sink - mx)
            probs = (e / denom).astype(BF16)
            yc = yc + jnp.where(head_of_lane == hq, _dot(probs, v2), 0.0)
        o_ref[r0:r0 + CHUNK, 2 * gw:3 * gw] = _rms(yc, gain[:, 2 * gw:3 * gw]).astype(o_ref.dtype)


def _local_mix(pa, pc, rel_bias, sinks, sgu_w, sgu_b_tile, avg, pool_bd, pool_scale,
               bucket, gain, batch, seq):
    m = pa.shape[0]
    rows = LOCAL_ROWS
    nq = seq // rows
    gw = GROUP_WIDTH
    n_heads = gw // HEAD_DIM

    def row_blk(b, j):
        return b * nq + j

    halo_per_blk = rows // POOL_HALO
    kv_per_blk = rows // CHUNK
    smem = pl.BlockSpec(memory_space=pltpu.SMEM)
    whole = pl.BlockSpec(memory_space=pltpu.VMEM)
    return pl.pallas_call(
        _local_mix_kernel,
        grid=(batch, nq),
        in_specs=[
            smem, smem,
            pl.BlockSpec((rows, 3 * gw), lambda b, j: (row_blk(b, j), 0)),
            pl.BlockSpec((POOL_HALO, gw),
                         lambda b, j: (jnp.maximum(row_blk(b, j) * halo_per_blk - 1, 0), 2)),
            pl.BlockSpec((rows, 3 * gw), lambda b, j: (row_blk(b, j), 0)),
            pl.BlockSpec((CHUNK, gw),
                         lambda b, j: (jnp.maximum(row_blk(b, j) * kv_per_blk - 1, 0), 1)),
            pl.BlockSpec((CHUNK, gw),
                         lambda b, j: (jnp.maximum(row_blk(b, j) * kv_per_blk - 1, 0), 2)),
            whole, whole, whole, whole, whole, whole, whole,
        ],
        out_specs=pl.BlockSpec((rows, 3 * gw), lambda b, j: (row_blk(b, j), 0)),
        out_shape=jax.ShapeDtypeStruct((m, 3 * gw), BF16),
        scratch_shapes=[pltpu.VMEM((n_heads, CHUNK, 2 * CHUNK), F32)],
        compiler_params=pltpu.CompilerParams(
            dimension_semantics=("arbitrary", "arbitrary"), vmem_limit_bytes=VMEM_LIMIT_MIX),
        name="local_mix",
    )(rel_bias, sinks, pa, pa, pc, pc, pc, sgu_w, sgu_b_tile, avg, pool_bd, pool_scale,
      bucket, gain)


def _softplus(z):
    return jnp.maximum(z, 0.0) + jnp.log(1.0 + jnp.exp(-jnp.abs(z)))


def _sb_attn_kernel(q_ref, k_ref, v_ref, upper2_ref, gain_ref, o_ref, qm_ref):
    j = pl.program_id(1)
    blk, gw = q_ref.shape
    half = gw // 2
    n_heads = gw // HEAD_DIM
    rows = n_heads * blk
    lane = lax.broadcasted_iota(jnp.int32, (1, gw), 1)
    lane_h = lax.broadcasted_iota(jnp.int32, (1, half), 1)

    q = q_ref[...]
    for h in range(n_heads):
        qm_ref[h * blk:(h + 1) * blk, :] = jnp.where(lane // HEAD_DIM == h, q, jnp.zeros_like(q))
    upper2 = upper2_ref[...]
    row = lax.broadcasted_iota(jnp.int32, (rows, blk), 0) & (blk - 1)
    col = lax.broadcasted_iota(jnp.int32, (rows, blk), 1)
    causal = col < row

    def tile(start, carry, diag):
        z = _dot_nt(qm_ref[...], k_ref[pl.ds(start, blk), :])
        sp = _softplus(z)
        if diag:
            sp = jnp.where(causal, sp, 0.0)
        hi, lo = _split_bf16(sp)
        tail = _dot(jnp.concatenate([hi, lo], axis=1), upper2)
        if carry is not None:
            tail = tail + carry
        w = jnp.exp(z - sp - tail)
        if diag:
            w = jnp.where(causal, w, 0.0)
        w = w.astype(BF16)
        pv = []
        for p in range(n_heads // 2):
            r = _dot(w[2 * p * blk:(2 * p + 2) * blk],
                     v_ref[pl.ds(start, blk), p * half:(p + 1) * half])
            pv.append(jnp.where(lane_h < HEAD_DIM, r[:blk], r[blk:]))
        return pv, jnp.sum(sp, axis=-1, keepdims=True)

    accs, carry = tile(pl.multiple_of(j * blk, blk), None, True)

    def body(i, state):
        accs, carry = state
        pv, rs = tile(pl.multiple_of((j - 1 - i) * blk, blk), carry, False)
        return [a + p for a, p in zip(accs, pv)], carry + rs

    accs, _ = lax.fori_loop(0, j, body, (accs, carry))
    y = jnp.concatenate(accs, axis=-1)
    o_ref[...] = _rms(y, gain_ref[...]).astype(o_ref.dtype)


def _sb_attn(pd, upper2, gain, batch, seq):
    m = pd.shape[0]
    gw = GROUP_WIDTH
    blk = SB_BLOCK
    nq = seq // blk
    whole = pl.BlockSpec(memory_space=pltpu.VMEM)
    return pl.pallas_call(
        _sb_attn_kernel,
        grid=(batch, nq),
        in_specs=[
            pl.BlockSpec((blk, gw), lambda b, j: (b * nq + j, 0)),
            pl.BlockSpec((seq, gw), lambda b, j: (b, 1)),
            pl.BlockSpec((seq, gw), lambda b, j: (b, 2)),
            whole, whole,
        ],
        out_specs=pl.BlockSpec((blk, gw), lambda b, j: (b * nq + j, 0)),
        out_shape=jax.ShapeDtypeStruct((m, gw), BF16),
        scratch_shapes=[pltpu.VMEM((gw // HEAD_DIM * blk, gw), BF16)],
        compiler_params=pltpu.CompilerParams(
            dimension_semantics=("arbitrary", "arbitrary"), vmem_limit_bytes=VMEM_LIMIT_MIX),
        name="sb_attn",
    )(pd, pd, pd, upper2, gain)


def _out_ffn_kernel(x_ref, yabc_ref, yd_ref, wout_ref, g_ref, wgu_ref, wdown_ref, gfin_ref,
                    o_ref, x2_ref, act_ref, *, final_norm):
    n_abc = yabc_ref.shape[1]
    d_ff = wdown_ref.shape[0]
    x2_ref[...] = (x_ref[...]
                   + _dot(yabc_ref[...], wout_ref[0:n_abc, :])
                   + _dot(yd_ref[...], wout_ref[n_abc:, :]))
    h = _rms(x2_ref[...], g_ref[...]).astype(BF16)
    for c in range(d_ff // FF_CHUNK):
        c0 = c * FF_CHUNK
        gate = _dot(h, wgu_ref[:, c0:c0 + FF_CHUNK])
        up = _dot(h, wgu_ref[:, d_ff + c0:d_ff + c0 + FF_CHUNK])
        act_ref[:, c0:c0 + FF_CHUNK] = (gate * (1.0 / (1.0 + jnp.exp(-gate))) * up).astype(BF16)
    out = x2_ref[...] + _dot(act_ref[...], wdown_ref[...])
    if final_norm:
        out = _rms(out, gfin_ref[...])
    o_ref[...] = out


def _out_ffn(x2d, yabc, yd, w_out, g, w_gu, w_down, g_final, final_norm):
    m, d = x2d.shape
    tm = DENSE_ROWS
    whole = pl.BlockSpec(memory_space=pltpu.VMEM)
    vec = pl.BlockSpec((1, d), lambda i: (0, 0))
    return pl.pallas_call(
        functools.partial(_out_ffn_kernel, final_norm=final_norm),
        grid=(m // tm,),
        in_specs=[
            pl.BlockSpec((tm, d), lambda i: (i, 0)),
            pl.BlockSpec((tm, yabc.shape[1]), lambda i: (i, 0)),
            pl.BlockSpec((tm, yd.shape[1]), lambda i: (i, 0)),
            whole, vec, whole, whole, vec,
        ],
        out_specs=pl.BlockSpec((tm, d), lambda i: (i, 0)),
        out_shape=jax.ShapeDtypeStruct((m, d), F32),
        scratch_shapes=[pltpu.VMEM((tm, d), F32), pltpu.VMEM((tm, w_down.shape[0]), BF16)],
        compiler_params=pltpu.CompilerParams(
            dimension_semantics=("arbitrary",), vmem_limit_bytes=VMEM_LIMIT_DENSE),
        name="out_ffn",
    )(x2d, yabc, yd, w_out, g, w_gu, w_down, g_final)


def _t5_bucket(dist):
    max_exact = N_BUCKETS // 2
    df = jnp.maximum(dist, 1).astype(F32)
    large = max_exact + (jnp.log(df / max_exact) / math.log(MAX_DISTANCE / max_exact)
                         * (N_BUCKETS - max_exact)).astype(jnp.int32)
    large = jnp.minimum(large, N_BUCKETS - 1)
    return jnp.where(dist < max_exact, dist, large)


def kernel(x, w_in, w_out, sgu_w, sgu_b, pool_w, pool_scale, swa_sinks, rel_bias, mix_out_gain,
           norm_mix, norm_ffn, w_gate_up, w_down, norm_final):
    batch, seq, d = x.shape
    depth = w_in.shape[0]
    gw = GROUP_WIDTH
    n_heads = gw // HEAD_DIM
    kv_heads = (w_in.shape[2] - 7 * gw) // (2 * HEAD_DIM)
    rep = n_heads // kv_heads
    assert seq % LOCAL_ROWS == 0 and seq % SB_BLOCK == 0 and (batch * seq) % DENSE_ROWS == 0

    head_id = np.arange(gw) // HEAD_DIM
    avg = jnp.asarray((head_id[:, None] == head_id[None, :]) / HEAD_DIM, BF16)
    upper = np.arange(SB_BLOCK)[:, None] > np.arange(SB_BLOCK)[None, :]
    upper2 = jnp.asarray(np.concatenate([upper, upper], axis=0), BF16)
    dist = (jnp.arange(CHUNK)[:, None] + CHUNK) - jnp.arange(2 * CHUNK)[None, :]
    bucket = _t5_bucket(jnp.clip(dist, 0, CHUNK - 1)).astype(jnp.int32)
    pool_group = gw // len(POOL_WINDOWS)
    scale = HEAD_DIM ** -0.5

    kvw = kv_heads * HEAD_DIM
    off = np.cumsum([0, gw, gw, gw, gw, kvw, kvw, gw, gw])

    def per_query_head(w, start):
        return [w[:, start + (hq // rep) * HEAD_DIM:start + (hq // rep + 1) * HEAD_DIM]
                for hq in range(n_heads)]

    x2d = x.reshape(batch * seq, d)
    for l in range(depth):
        w = w_in[l]
        w_l = jnp.concatenate(
            [w[:, :off[3]], w[:, off[3]:off[4]] * scale]
            + per_query_head(w, off[4]) + per_query_head(w, off[5])
            + [w[:, off[6]:off[7]] * scale, w[:, off[7]:]], axis=1).astype(BF16)
        pa, pc, pd = _norm_proj(x2d, norm_mix[l].reshape(1, d), w_l, 3 * gw, 3 * gw, 3 * gw)

        pool_bd = jnp.zeros((gw, gw), F32)
        for g in range(len(POOL_WINDOWS)):
            pool_bd = lax.dynamic_update_slice(pool_bd, pool_w[l, g], (g * pool_group, g * pool_group))
        sgu_b_tile = jnp.repeat(sgu_b[l].T, HEAD_DIM, axis=1)
        gain = mix_out_gain[l].reshape(1, 4 * gw)
        yabc = _local_mix(pa, pc, rel_bias, swa_sinks[l], sgu_w[l], sgu_b_tile, avg,
                          pool_bd.astype(BF16), pool_scale[l].reshape(1, gw), bucket,
                          gain[:, :3 * gw], batch, seq)
        yd = _sb_attn(pd, upper2, gain[:, 3 * gw:], batch, seq)

        x2d = _out_ffn(x2d, yabc, yd, w_out[l].astype(BF16), norm_ffn[l].reshape(1, d),
                       w_gate_up[l].astype(BF16), w_down[l].astype(BF16),
                       norm_final.reshape(1, d), l == depth - 1)
    return x2d.reshape(batch, seq, d)
```

```python
import functools
import math

import numpy as np
import jax
import jax.numpy as jnp
from jax import lax
from jax.experimental import pallas as pl
from jax.experimental.pallas import tpu as pltpu

F32 = jnp.float32
BF16 = jnp.bfloat16

HEAD_DIM = 64
GROUP_WIDTH = 256
CHUNK = 128
POOL_WINDOWS = (2, 4, 8, 16)
POOL_HALO = 16
N_BUCKETS = 32
MAX_DISTANCE = 128
EPS = 1e-6
MASK_VALUE = -1e30

LOCAL_ROWS = 512
SB_BLOCK = 256
DENSE_ROWS = 512
FF_CHUNK = 256
VMEM_LIMIT_DENSE = 52 * 1024 * 1024
VMEM_LIMIT_MIX = 40 * 1024 * 1024


def _rms(x, g):
    return x * lax.rsqrt(jnp.mean(x * x, axis=-1, keepdims=True) + EPS) * g


def _gelu_tanh(x):
    inner = math.sqrt(2.0 / math.pi) * (x + 0.044715 * (x * x * x))
    return x * (0.5 * (1.0 + jnp.tanh(inner)))


def _split_bf16(x):
    hi = x.astype(BF16)
    lo = (x - hi.astype(F32)).astype(BF16)
    return hi, lo


def _dot(a, b):
    return jnp.dot(a, b, preferred_element_type=F32)


def _dot_nt(a, b):
    return lax.dot_general(a, b, (((1,), (1,)), ((), ())), preferred_element_type=F32)


def _norm_proj_kernel(x_ref, g_ref, w_ref, pa_ref, pc_ref, pd_ref):
    h = _rms(x_ref[...], g_ref[...]).astype(BF16)
    y = _dot(h, w_ref[...])
    na = pa_ref.shape[1]
    nc = pc_ref.shape[1]
    pa_ref[...] = y[:, :na]
    pc_ref[...] = y[:, na:na + nc].astype(BF16)
    pd_ref[...] = y[:, na + nc:].astype(BF16)


def _norm_proj(x2d, g, w, na, nc, nd):
    m, d = x2d.shape
    tm = DENSE_ROWS
    return pl.pallas_call(
        _norm_proj_kernel,
        grid=(m // tm,),
        in_specs=[
            pl.BlockSpec((tm, d), lambda i: (i, 0)),
            pl.BlockSpec((1, d), lambda i: (0, 0)),
            pl.BlockSpec(memory_space=pltpu.VMEM),
        ],
        out_specs=[
            pl.BlockSpec((tm, na), lambda i: (i, 0)),
            pl.BlockSpec((tm, nc), lambda i: (i, 0)),
            pl.BlockSpec((tm, nd), lambda i: (i, 0)),
        ],
        out_shape=[
            jax.ShapeDtypeStruct((m, na), F32),
            jax.ShapeDtypeStruct((m, nc), BF16),
            jax.ShapeDtypeStruct((m, nd), BF16),
        ],
        compiler_params=pltpu.CompilerParams(
            dimension_semantics=("arbitrary",), vmem_limit_bytes=VMEM_LIMIT_MIX),
        name="norm_proj",
    )(x2d, g, w)


def _mask_heads(x, n_heads):
    lane = lax.broadcasted_iota(jnp.int32, (1, x.shape[1]), 1)
    return jnp.concatenate(
        [jnp.where(lane // HEAD_DIM == h, x, jnp.zeros_like(x)) for h in range(n_heads)], axis=0)


def _local_mix_kernel(relb_ref, sink_ref,
                      pa_ref, halo_ref, pc_ref, kh_ref, vh_ref,
                      sguw_ref, sgub_ref, avg_ref, poolw_ref, pscale_ref,
                      bucket_ref, gain_ref,
                      o_ref,
                      bias_ref, wcat_ref):
    b = pl.program_id(0)
    j = pl.program_id(1)
    rows = pa_ref.shape[0]
    gw = GROUP_WIDTH
    n_heads = gw // HEAD_DIM
    n_chunks = rows // CHUNK
    lane = lax.broadcasted_iota(jnp.int32, (1, gw), 1)

    @pl.when(jnp.logical_and(b == 0, j == 0))
    def _():
        bucket = bucket_ref[...]
        tril = (lax.broadcasted_iota(jnp.int32, (CHUNK, CHUNK), 1)
                <= lax.broadcasted_iota(jnp.int32, (CHUNK, CHUNK), 0))
        for h in range(n_heads):
            acc = jnp.zeros(bucket.shape, F32)
            for bk in range(N_BUCKETS):
                acc = jnp.where(bucket == bk, relb_ref[bk, h], acc)
            bias_ref[h * CHUNK:(h + 1) * CHUNK, :] = acc
            wcat_ref[:, h * CHUNK:(h + 1) * CHUNK] = jnp.where(tril, sguw_ref[h], 0.0).astype(BF16)

    gain = gain_ref[...]

    p = pa_ref[:, 2 * gw:3 * gw]
    halo = jnp.where(j == 0, 0.0, halo_ref[...])
    ext = jnp.concatenate([halo, p], axis=0)
    sums = []
    run = ext
    for shift in (1, 2, 4, 8):
        run = run + pltpu.roll(run, shift, 0)
        sums.append(run[POOL_HALO:])
    group = lane // (gw // len(POOL_WINDOWS))
    wsum = jnp.where(group == 0, sums[0],
                     jnp.where(group == 1, sums[1],
                               jnp.where(group == 2, sums[2], sums[3])))
    win = jnp.where(group == 0, POOL_WINDOWS[0],
                    jnp.where(group == 1, POOL_WINDOWS[1],
                              jnp.where(group == 2, POOL_WINDOWS[2], POOL_WINDOWS[3])))
    t_seq = j * rows + lax.broadcasted_iota(jnp.int32, (rows, 1), 0)
    count = jnp.minimum(t_seq + 1, win).astype(F32)
    yb = wsum / count - p
    yb = _dot(yb.astype(BF16), poolw_ref[...]) * pscale_ref[...]
    o_ref[:, gw:2 * gw] = _rms(yb, gain[:, gw:2 * gw]).astype(o_ref.dtype)

    avg = avg_ref[...]
    u = _gelu_tanh(pa_ref[:, 0:gw])
    v = _gelu_tanh(pa_ref[:, gw:2 * gw])
    v_hi, v_lo = _split_bf16(v)
    vc = v - _dot(jnp.concatenate([v_hi, v_lo], axis=1), avg)
    sq_hi, sq_lo = _split_bf16(vc * vc)
    var = _dot(jnp.concatenate([sq_hi, sq_lo], axis=1), avg)
    vn = (vc * lax.rsqrt(var + EPS)).astype(BF16)
    rhs = jnp.concatenate(
        [_mask_heads(vn[c * CHUNK:(c + 1) * CHUNK], n_heads) for c in range(n_chunks)], axis=1)
    mix = _dot(wcat_ref[...], rhs)
    for c in range(n_chunks):
        r0 = c * CHUNK
        ya = u[r0:r0 + CHUNK] * (mix[:, c * gw:(c + 1) * gw] + sgub_ref[...])
        o_ref[r0:r0 + CHUNK, 0:gw] = _rms(ya, gain[:, 0:gw]).astype(o_ref.dtype)

    stacked = (n_heads * CHUNK, 2 * CHUNK)
    qi = lax.broadcasted_iota(jnp.int32, stacked, 0) & (CHUNK - 1)
    ki = lax.broadcasted_iota(jnp.int32, stacked, 1)
    in_window = jnp.logical_and(ki > qi, ki <= qi + CHUNK)
    head_of_row = lax.broadcasted_iota(jnp.int32, (n_heads * CHUNK, 1), 0) // CHUNK
    sink = jnp.zeros((n_heads * CHUNK, 1), F32)
    for h in range(n_heads):
        sink = jnp.where(head_of_row == h, sink_ref[h], sink)
    bias = bias_ref[...]
    for c in range(n_chunks):
        r0 = c * CHUNK
        if c == 0:
            k_prev, v_prev = kh_ref[...], vh_ref[...]
            mask = jnp.logical_and(in_window, jnp.logical_or(ki >= CHUNK, j > 0))
        else:
            k_prev = pc_ref[r0 - CHUNK:r0, gw:2 * gw]
            v_prev = pc_ref[r0 - CHUNK:r0, 2 * gw:3 * gw]
            mask = in_window
        k2 = jnp.concatenate([k_prev, pc_ref[r0:r0 + CHUNK, gw:2 * gw]], axis=0)
        v2 = jnp.concatenate([v_prev, pc_ref[r0:r0 + CHUNK, 2 * gw:3 * gw]], axis=0)
        qm = _mask_heads(pc_ref[r0:r0 + CHUNK, 0:gw], n_heads)
        logits = jnp.where(mask, _dot_nt(qm, k2) + bias, MASK_VALUE)
        mx = jnp.maximum(jnp.max(logits, axis=-1, keepdims=True), sink)
        e = jnp.exp(logits - mx)
        denom = jnp.sum(e, axis=-1, keepdims=True) + jnp.exp(sink - mx)
        probs = (e * (1.0 / denom)).astype(BF16)
        pcat = jnp.concatenate([probs[h * CHUNK:(h + 1) * CHUNK] for h in range(n_heads)], axis=1)
        yc = _dot(pcat, _mask_heads(v2, n_heads))
        o_ref[r0:r0 + CHUNK, 2 * gw:3 * gw] = _rms(yc, gain[:, 2 * gw:3 * gw]).astype(o_ref.dtype)


def _local_mix(pa, pc, rel_bias, sinks, sgu_w, sgu_b_tile, avg, pool_bd, pool_scale,
               bucket, gain, batch, seq):
    m = pa.shape[0]
    rows = LOCAL_ROWS
    nq = seq // rows
    gw = GROUP_WIDTH
    n_heads = gw // HEAD_DIM

    def row_blk(b, j):
        return b * nq + j

    halo_per_blk = rows // POOL_HALO
    kv_per_blk = rows // CHUNK
    smem = pl.BlockSpec(memory_space=pltpu.SMEM)
    whole = pl.BlockSpec(memory_space=pltpu.VMEM)
    return pl.pallas_call(
        _local_mix_kernel,
        grid=(batch, nq),
        in_specs=[
            smem, smem,
            pl.BlockSpec((rows, 3 * gw), lambda b, j: (row_blk(b, j), 0)),
            pl.BlockSpec((POOL_HALO, gw),
                         lambda b, j: (jnp.maximum(row_blk(b, j) * halo_per_blk - 1, 0), 2)),
            pl.BlockSpec((rows, 3 * gw), lambda b, j: (row_blk(b, j), 0)),
            pl.BlockSpec((CHUNK, gw),
                         lambda b, j: (jnp.maximum(row_blk(b, j) * kv_per_blk - 1, 0), 1)),
            pl.BlockSpec((CHUNK, gw),
                         lambda b, j: (jnp.maximum(row_blk(b, j) * kv_per_blk - 1, 0), 2)),
            whole, whole, whole, whole, whole, whole, whole,
        ],
        out_specs=pl.BlockSpec((rows, 3 * gw), lambda b, j: (row_blk(b, j), 0)),
        out_shape=jax.ShapeDtypeStruct((m, 3 * gw), BF16),
        scratch_shapes=[pltpu.VMEM((n_heads * CHUNK, 2 * CHUNK), F32),
                        pltpu.VMEM((CHUNK, n_heads * CHUNK), BF16)],
        compiler_params=pltpu.CompilerParams(
            dimension_semantics=("arbitrary", "arbitrary"), vmem_limit_bytes=VMEM_LIMIT_MIX),
        name="local_mix",
    )(rel_bias, sinks, pa, pa, pc, pc, pc, sgu_w, sgu_b_tile, avg, pool_bd, pool_scale,
      bucket, gain)


def _softplus(z):
    neg_abs = lax.bitcast_convert_type(
        lax.bitcast_convert_type(z, jnp.uint32) | jnp.uint32(0x80000000), F32)
    return jnp.maximum(z, 0.0) + jnp.log(1.0 + jnp.exp(neg_abs))


def _sb_attn_kernel(q_ref, k_ref, v_ref, upper2_ref, gain_ref, o_ref, qm_ref, acc_ref, carry_ref):
    j = pl.program_id(1)
    blk, gw = q_ref.shape
    half = gw // 2
    n_heads = gw // HEAD_DIM
    rows = n_heads * blk
    lane = lax.broadcasted_iota(jnp.int32, (1, gw), 1)
    lane_h = lax.broadcasted_iota(jnp.int32, (1, half), 1)

    q = q_ref[...]
    for h in range(n_heads):
        qm_ref[h * blk:(h + 1) * blk, :] = jnp.where(lane // HEAD_DIM == h, q, jnp.zeros_like(q))
    upper2 = upper2_ref[...]
    row = lax.broadcasted_iota(jnp.int32, (rows, blk), 0) & (blk - 1)
    col = lax.broadcasted_iota(jnp.int32, (rows, blk), 1)
    causal = col < row

    def tile(t, carry, diag):
        start = pl.multiple_of((j - t) * blk, blk)
        z = _dot_nt(qm_ref[...], k_ref[pl.ds(start, blk), :])
        sp = _softplus(z)
        log_sig = z - sp
        if diag:
            sp = jnp.where(causal, sp, 0.0)
        hi, lo = _split_bf16(sp)
        tail = _dot(jnp.concatenate([hi, lo], axis=1), upper2)
        if carry is not None:
            tail = tail + jnp.concatenate([carry, carry], axis=1)
        w = jnp.exp(log_sig - tail)
        if diag:
            w = jnp.where(causal, w, 0.0)
        w = w.astype(BF16)
        rowsum = jnp.broadcast_to(jnp.sum(sp, axis=-1, keepdims=True), (rows, half))
        pv = []
        for p in range(n_heads // 2):
            r = _dot(w[2 * p * blk:(2 * p + 2) * blk],
                     v_ref[pl.ds(start, blk), p * half:(p + 1) * half])
            pv.append(jnp.where(lane_h < HEAD_DIM, r[:blk], r[blk:]))
        return pv, rowsum if carry is None else carry + rowsum

    pv, carry = tile(0, None, True)
    carry_ref[...] = carry
    for p in range(n_heads // 2):
        acc_ref[p] = pv[p]

    def pair(i, _):
        pv_a, carry = tile(2 * i + 1, carry_ref[...], False)
        pv_b, carry = tile(2 * i + 2, carry, False)
        carry_ref[...] = carry
        for p in range(n_heads // 2):
            acc_ref[p] += pv_a[p] + pv_b[p]
        return 0

    lax.fori_loop(0, j // 2, pair, 0)

    @pl.when(j % 2 == 1)
    def _():
        pv, _ = tile(j, carry_ref[...], False)
        for p in range(n_heads // 2):
            acc_ref[p] += pv[p]

    y = jnp.concatenate([acc_ref[p] for p in range(n_heads // 2)], axis=-1)
    o_ref[...] = _rms(y, gain_ref[...]).astype(o_ref.dtype)


def _sb_attn(pd, upper2, gain, batch, seq):
    m = pd.shape[0]
    gw = GROUP_WIDTH
    blk = SB_BLOCK
    nq = seq // blk
    n_heads = gw // HEAD_DIM
    whole = pl.BlockSpec(memory_space=pltpu.VMEM)
    return pl.pallas_call(
        _sb_attn_kernel,
        grid=(batch, nq),
        in_specs=[
            pl.BlockSpec((blk, gw), lambda b, j: (b * nq + j, 0)),
            pl.BlockSpec((seq, gw), lambda b, j: (b, 1)),
            pl.BlockSpec((seq, gw), lambda b, j: (b, 2)),
            whole, whole,
        ],
        out_specs=pl.BlockSpec((blk, gw), lambda b, j: (b * nq + j, 0)),
        out_shape=jax.ShapeDtypeStruct((m, gw), BF16),
        scratch_shapes=[pltpu.VMEM((n_heads * blk, gw), BF16),
                        pltpu.VMEM((n_heads // 2, blk, gw // 2), F32),
                        pltpu.VMEM((n_heads * blk, gw // 2), F32)],
        compiler_params=pltpu.CompilerParams(
            dimension_semantics=("arbitrary", "arbitrary"), vmem_limit_bytes=VMEM_LIMIT_MIX),
        name="sb_attn",
    )(pd, pd, pd, upper2, gain)


def _out_ffn_kernel(x_ref, yabc_ref, yd_ref, wout_ref, g_ref, wgu_ref, wdown_ref, gfin_ref,
                    o_ref, x2_ref, act_ref, *, final_norm):
    n_abc = yabc_ref.shape[1]
    d_ff = wdown_ref.shape[0]
    x2_ref[...] = (x_ref[...]
                   + _dot(yabc_ref[...], wout_ref[0:n_abc, :])
                   + _dot(yd_ref[...], wout_ref[n_abc:, :]))
    h = _rms(x2_ref[...], g_ref[...]).astype(BF16)
    for c in range(d_ff // FF_CHUNK):
        c0 = c * FF_CHUNK
        gate = _dot(h, wgu_ref[:, c0:c0 + FF_CHUNK])
        up = _dot(h, wgu_ref[:, d_ff + c0:d_ff + c0 + FF_CHUNK])
        act_ref[:, c0:c0 + FF_CHUNK] = (gate * (1.0 / (1.0 + jnp.exp(-gate))) * up).astype(BF16)
    out = x2_ref[...] + _dot(act_ref[...], wdown_ref[...])
    if final_norm:
        out = _rms(out, gfin_ref[...])
    o_ref[...] = out


def _out_ffn(x2d, yabc, yd, w_out, g, w_gu, w_down, g_final, final_norm):
    m, d = x2d.shape
    tm = DENSE_ROWS
    whole = pl.BlockSpec(memory_space=pltpu.VMEM)
    vec = pl.BlockSpec((1, d), lambda i: (0, 0))
    return pl.pallas_call(
        functools.partial(_out_ffn_kernel, final_norm=final_norm),
        grid=(m // tm,),
        in_specs=[
            pl.BlockSpec((tm, d), lambda i: (i, 0)),
            pl.BlockSpec((tm, yabc.shape[1]), lambda i: (i, 0)),
            pl.BlockSpec((tm, yd.shape[1]), lambda i: (i, 0)),
            whole, vec, whole, whole, vec,
        ],
        out_specs=pl.BlockSpec((tm, d), lambda i: (i, 0)),
        out_shape=jax.ShapeDtypeStruct((m, d), F32),
        scratch_shapes=[pltpu.VMEM((tm, d), F32), pltpu.VMEM((tm, w_down.shape[0]), BF16)],
        compiler_params=pltpu.CompilerParams(
            dimension_semantics=("arbitrary",), vmem_limit_bytes=VMEM_LIMIT_DENSE),
        name="out_ffn",
    )(x2d, yabc, yd, w_out, g, w_gu, w_down, g_final)


def _t5_bucket(dist):
    max_exact = N_BUCKETS // 2
    df = jnp.maximum(dist, 1).astype(F32)
    large = max_exact + (jnp.log(df / max_exact) / math.log(MAX_DISTANCE / max_exact)
                         * (N_BUCKETS - max_exact)).astype(jnp.int32)
    large = jnp.minimum(large, N_BUCKETS - 1)
    return jnp.where(dist < max_exact, dist, large)


def kernel(x, w_in, w_out, sgu_w, sgu_b, pool_w, pool_scale, swa_sinks, rel_bias, mix_out_gain,
           norm_mix, norm_ffn, w_gate_up, w_down, norm_final):
    batch, seq, d = x.shape
    depth = w_in.shape[0]
    gw = GROUP_WIDTH
    n_heads = gw // HEAD_DIM
    kv_heads = (w_in.shape[2] - 7 * gw) // (2 * HEAD_DIM)
    rep = n_heads // kv_heads
    assert seq % LOCAL_ROWS == 0 and seq % SB_BLOCK == 0 and (batch * seq) % DENSE_ROWS == 0

    head_id = np.arange(gw) // HEAD_DIM
    avg = (head_id[:, None] == head_id[None, :]) / HEAD_DIM
    avg2 = jnp.asarray(np.concatenate([avg, avg], axis=0), BF16)
    upper = np.arange(SB_BLOCK)[:, None] > np.arange(SB_BLOCK)[None, :]
    upper2 = jnp.asarray(np.concatenate([upper, upper], axis=0), BF16)
    dist = (jnp.arange(CHUNK)[:, None] + CHUNK) - jnp.arange(2 * CHUNK)[None, :]
    bucket = _t5_bucket(jnp.clip(dist, 0, CHUNK - 1)).astype(jnp.int32)
    pool_group = gw // len(POOL_WINDOWS)
    scale = HEAD_DIM ** -0.5

    kvw = kv_heads * HEAD_DIM
    off = np.cumsum([0, gw, gw, gw, gw, kvw, kvw, gw, gw])

    def per_query_head(w, start):
        return [w[:, start + (hq // rep) * HEAD_DIM:start + (hq // rep + 1) * HEAD_DIM]
                for hq in range(n_heads)]

    x2d = x.reshape(batch * seq, d)
    for l in range(depth):
        w = w_in[l]
        w_l = jnp.concatenate(
            [w[:, :off[3]], w[:, off[3]:off[4]] * scale]
            + per_query_head(w, off[4]) + per_query_head(w, off[5])
            + [w[:, off[6]:off[7]] * scale, w[:, off[7]:]], axis=1).astype(BF16)
        pa, pc, pd = _norm_proj(x2d, norm_mix[l].reshape(1, d), w_l, 3 * gw, 3 * gw, 3 * gw)

        pool_bd = jnp.zeros((gw, gw), F32)
        for g in range(len(POOL_WINDOWS)):
            pool_bd = lax.dynamic_update_slice(pool_bd, pool_w[l, g], (g * pool_group, g * pool_group))
        sgu_b_tile = jnp.repeat(sgu_b[l].T, HEAD_DIM, axis=1)
        gain = mix_out_gain[l].reshape(1, 4 * gw)
        yabc = _local_mix(pa, pc, rel_bias, swa_sinks[l], sgu_w[l], sgu_b_tile, avg2,
                          pool_bd.astype(BF16), pool_scale[l].reshape(1, gw), bucket,
                          gain[:, :3 * gw], batch, seq)
        yd = _sb_attn(pd, upper2, gain[:, 3 * gw:], batch, seq)

        x2d = _out_ffn(x2d, yabc, yd, w_out[l].astype(BF16), norm_ffn[l].reshape(1, d),
                       w_gate_up[l].astype(BF16), w_down[l].astype(BF16),
                       norm_final.reshape(1, d), l == depth - 1)
    return x2d.reshape(batch, seq, d)
```

```python
import functools
import math

import numpy as np
import jax
import jax.numpy as jnp
from jax import lax
from jax.experimental import pallas as pl
from jax.experimental.pallas import tpu as pltpu

F32 = jnp.float32
BF16 = jnp.bfloat16

HEAD_DIM = 64
GROUP_WIDTH = 256
CHUNK = 128
POOL_WINDOWS = (2, 4, 8, 16)
POOL_HALO = 16
N_BUCKETS = 32
MAX_DISTANCE = 128
EPS = 1e-6
MASK_VALUE = -1e30
EXP_UNDERFLOW = 104.0

LOCAL_ROWS = 512
SB_BLOCK = 256
DENSE_ROWS = 512
FF_CHUNK = 256
VMEM_LIMIT_DENSE = 52 * 1024 * 1024
VMEM_LIMIT_MIX = 40 * 1024 * 1024


def _rms(x, g):
    return x * lax.rsqrt(jnp.mean(x * x, axis=-1, keepdims=True) + EPS) * g


def _gelu_tanh(x):
    inner = math.sqrt(2.0 / math.pi) * (x + 0.044715 * (x * x * x))
    return x * (0.5 * (1.0 + jnp.tanh(inner)))


def _split_bf16(x):
    hi = x.astype(BF16)
    lo = (x - hi.astype(F32)).astype(BF16)
    return hi, lo


def _dot(a, b):
    return jnp.dot(a, b, preferred_element_type=F32)


def _dot_nt(a, b):
    return lax.dot_general(a, b, (((1,), (1,)), ((), ())), preferred_element_type=F32)


def _norm_proj_kernel(x_ref, g_ref, w_ref, pa_ref, pc_ref, pd_ref):
    h = _rms(x_ref[...], g_ref[...]).astype(BF16)
    y = _dot(h, w_ref[...])
    na = pa_ref.shape[1]
    nc = pc_ref.shape[1]
    pa_ref[...] = y[:, :na]
    pc_ref[...] = y[:, na:na + nc].astype(BF16)
    pd_ref[...] = y[:, na + nc:].astype(BF16)


def _norm_proj(x2d, g, w, na, nc, nd):
    m, d = x2d.shape
    tm = DENSE_ROWS
    return pl.pallas_call(
        _norm_proj_kernel,
        grid=(m // tm,),
        in_specs=[
            pl.BlockSpec((tm, d), lambda i: (i, 0)),
            pl.BlockSpec((1, d), lambda i: (0, 0)),
            pl.BlockSpec(memory_space=pltpu.VMEM),
        ],
        out_specs=[
            pl.BlockSpec((tm, na), lambda i: (i, 0)),
            pl.BlockSpec((tm, nc), lambda i: (i, 0)),
            pl.BlockSpec((tm, nd), lambda i: (i, 0)),
        ],
        out_shape=[
            jax.ShapeDtypeStruct((m, na), F32),
            jax.ShapeDtypeStruct((m, nc), BF16),
            jax.ShapeDtypeStruct((m, nd), BF16),
        ],
        compiler_params=pltpu.CompilerParams(
            dimension_semantics=("arbitrary",), vmem_limit_bytes=VMEM_LIMIT_MIX),
        name="norm_proj",
    )(x2d, g, w)


def _mask_heads(x, n_heads):
    lane = lax.broadcasted_iota(jnp.int32, (1, x.shape[1]), 1)
    return jnp.concatenate(
        [jnp.where(lane // HEAD_DIM == h, x, jnp.zeros_like(x)) for h in range(n_heads)], axis=0)


def _local_mix_kernel(relb_ref, sink_ref,
                      pa_ref, halo_ref, pc_ref, kh_ref, vh_ref,
                      sguw_ref, sgub_ref, avg_ref, poolw_ref, pscale_ref,
                      bucket_ref, gain_ref,
                      o_ref,
                      bias_ref, wcat_ref):
    b = pl.program_id(0)
    j = pl.program_id(1)
    rows = pa_ref.shape[0]
    gw = GROUP_WIDTH
    n_heads = gw // HEAD_DIM
    n_chunks = rows // CHUNK
    lane = lax.broadcasted_iota(jnp.int32, (1, gw), 1)

    @pl.when(jnp.logical_and(b == 0, j == 0))
    def _():
        bucket = bucket_ref[...]
        tril = (lax.broadcasted_iota(jnp.int32, (CHUNK, CHUNK), 1)
                <= lax.broadcasted_iota(jnp.int32, (CHUNK, CHUNK), 0))
        for h in range(n_heads):
            acc = jnp.zeros(bucket.shape, F32)
            for bk in range(N_BUCKETS):
                acc = jnp.where(bucket == bk, relb_ref[bk, h], acc)
            bias_ref[h * CHUNK:(h + 1) * CHUNK, :] = acc
            wcat_ref[:, h * CHUNK:(h + 1) * CHUNK] = jnp.where(tril, sguw_ref[h], 0.0).astype(BF16)

    gain = gain_ref[...]

    p = pa_ref[:, 2 * gw:3 * gw]
    halo = jnp.where(j == 0, 0.0, halo_ref[...])
    ext = jnp.concatenate([halo, p], axis=0)
    sums = []
    run = ext
    for shift in (1, 2, 4, 8):
        run = run + pltpu.roll(run, shift, 0)
        sums.append(run[POOL_HALO:])
    group = lane // (gw // len(POOL_WINDOWS))
    wsum = jnp.where(group == 0, sums[0],
                     jnp.where(group == 1, sums[1],
                               jnp.where(group == 2, sums[2], sums[3])))
    win = jnp.where(group == 0, POOL_WINDOWS[0],
                    jnp.where(group == 1, POOL_WINDOWS[1],
                              jnp.where(group == 2, POOL_WINDOWS[2], POOL_WINDOWS[3])))
    t_seq = j * rows + lax.broadcasted_iota(jnp.int32, (rows, 1), 0)
    count = jnp.minimum(t_seq + 1, win).astype(F32)
    yb = wsum / count - p
    yb = _dot(yb.astype(BF16), poolw_ref[...]) * pscale_ref[...]
    o_ref[:, gw:2 * gw] = _rms(yb, gain[:, gw:2 * gw]).astype(o_ref.dtype)

    avg = avg_ref[...]
    u = _gelu_tanh(pa_ref[:, 0:gw])
    v = _gelu_tanh(pa_ref[:, gw:2 * gw])
    v_hi, v_lo = _split_bf16(v)
    vc = v - _dot(jnp.concatenate([v_hi, v_lo], axis=1), avg)
    sq_hi, sq_lo = _split_bf16(vc * vc)
    var = _dot(jnp.concatenate([sq_hi, sq_lo], axis=1), avg)
    vn = (vc * lax.rsqrt(var + EPS)).astype(BF16)
    rhs = jnp.concatenate(
        [_mask_heads(vn[c * CHUNK:(c + 1) * CHUNK], n_heads) for c in range(n_chunks)], axis=1)
    mix = _dot(wcat_ref[...], rhs)
    for c in range(n_chunks):
        r0 = c * CHUNK
        ya = u[r0:r0 + CHUNK] * (mix[:, c * gw:(c + 1) * gw] + sgub_ref[...])
        o_ref[r0:r0 + CHUNK, 0:gw] = _rms(ya, gain[:, 0:gw]).astype(o_ref.dtype)

    stacked = (n_heads * CHUNK, 2 * CHUNK)
    qi = lax.broadcasted_iota(jnp.int32, stacked, 0) & (CHUNK - 1)
    ki = lax.broadcasted_iota(jnp.int32, stacked, 1)
    in_window = jnp.logical_and(ki > qi, ki <= qi + CHUNK)
    head_of_row = lax.broadcasted_iota(jnp.int32, (n_heads * CHUNK, 1), 0) // CHUNK
    sink = jnp.zeros((n_heads * CHUNK, 1), F32)
    for h in range(n_heads):
        sink = jnp.where(head_of_row == h, sink_ref[h], sink)
    bias = bias_ref[...]
    for c in range(n_chunks):
        r0 = c * CHUNK
        if c == 0:
            k_prev, v_prev = kh_ref[...], vh_ref[...]
            mask = jnp.logical_and(in_window, jnp.logical_or(ki >= CHUNK, j > 0))
        else:
            k_prev = pc_ref[r0 - CHUNK:r0, gw:2 * gw]
            v_prev = pc_ref[r0 - CHUNK:r0, 2 * gw:3 * gw]
            mask = in_window
        k2 = jnp.concatenate([k_prev, pc_ref[r0:r0 + CHUNK, gw:2 * gw]], axis=0)
        v2 = jnp.concatenate([v_prev, pc_ref[r0:r0 + CHUNK, 2 * gw:3 * gw]], axis=0)
        qm = _mask_heads(pc_ref[r0:r0 + CHUNK, 0:gw], n_heads)
        logits = jnp.where(mask, _dot_nt(qm, k2) + bias, MASK_VALUE)
        mx = jnp.maximum(jnp.max(logits, axis=-1, keepdims=True), sink)
        e = jnp.exp(logits - mx)
        denom = jnp.sum(e, axis=-1, keepdims=True) + jnp.exp(sink - mx)
        probs = (e * (1.0 / denom)).astype(BF16)
        pcat = jnp.concatenate([probs[h * CHUNK:(h + 1) * CHUNK] for h in range(n_heads)], axis=1)
        yc = _dot(pcat, _mask_heads(v2, n_heads))
        o_ref[r0:r0 + CHUNK, 2 * gw:3 * gw] = _rms(yc, gain[:, 2 * gw:3 * gw]).astype(o_ref.dtype)


def _local_mix(pa, pc, rel_bias, sinks, sgu_w, sgu_b_tile, avg, pool_bd, pool_scale,
               bucket, gain, batch, seq):
    m = pa.shape[0]
    rows = LOCAL_ROWS
    nq = seq // rows
    gw = GROUP_WIDTH
    n_heads = gw // HEAD_DIM

    def row_blk(b, j):
        return b * nq + j

    halo_per_blk = rows // POOL_HALO
    kv_per_blk = rows // CHUNK
    smem = pl.BlockSpec(memory_space=pltpu.SMEM)
    whole = pl.BlockSpec(memory_space=pltpu.VMEM)
    return pl.pallas_call(
        _local_mix_kernel,
        grid=(batch, nq),
        in_specs=[
            smem, smem,
            pl.BlockSpec((rows, 3 * gw), lambda b, j: (row_blk(b, j), 0)),
            pl.BlockSpec((POOL_HALO, gw),
                         lambda b, j: (jnp.maximum(row_blk(b, j) * halo_per_blk - 1, 0), 2)),
            pl.BlockSpec((rows, 3 * gw), lambda b, j: (row_blk(b, j), 0)),
            pl.BlockSpec((CHUNK, gw),
                         lambda b, j: (jnp.maximum(row_blk(b, j) * kv_per_blk - 1, 0), 1)),
            pl.BlockSpec((CHUNK, gw),
                         lambda b, j: (jnp.maximum(row_blk(b, j) * kv_per_blk - 1, 0), 2)),
            whole, whole, whole, whole, whole, whole, whole,
        ],
        out_specs=pl.BlockSpec((rows, 3 * gw), lambda b, j: (row_blk(b, j), 0)),
        out_shape=jax.ShapeDtypeStruct((m, 3 * gw), BF16),
        scratch_shapes=[pltpu.VMEM((n_heads * CHUNK, 2 * CHUNK), F32),
                        pltpu.VMEM((CHUNK, n_heads * CHUNK), BF16)],
        compiler_params=pltpu.CompilerParams(
            dimension_semantics=("arbitrary", "arbitrary"), vmem_limit_bytes=VMEM_LIMIT_MIX),
        name="local_mix",
    )(rel_bias, sinks, pa, pa, pc, pc, pc, sgu_w, sgu_b_tile, avg, pool_bd, pool_scale,
      bucket, gain)


def _softplus(z):
    neg_abs = lax.bitcast_convert_type(
        lax.bitcast_convert_type(z, jnp.uint32) | jnp.uint32(0x80000000), F32)
    return jnp.maximum(z, 0.0) + jnp.log(1.0 + jnp.exp(neg_abs))


def _sb_attn_kernel(q_ref, k_ref, v_ref, upper2_ref, gain_ref, o_ref,
                    qm_ref, acc_ref, carry_ref, cmin_ref):
    j = pl.program_id(1)
    blk, gw = q_ref.shape
    half = gw // 2
    n_heads = gw // HEAD_DIM
    rows = n_heads * blk
    lane = lax.broadcasted_iota(jnp.int32, (1, gw), 1)
    lane_h = lax.broadcasted_iota(jnp.int32, (1, half), 1)

    q = q_ref[...]
    for h in range(n_heads):
        qm_ref[h * blk:(h + 1) * blk, :] = jnp.where(lane // HEAD_DIM == h, q, jnp.zeros_like(q))
    upper2 = upper2_ref[...]
    row = lax.broadcasted_iota(jnp.int32, (rows, blk), 0) & (blk - 1)
    col = lax.broadcasted_iota(jnp.int32, (rows, blk), 1)
    causal = col < row

    def tile(t, carry, diag):
        start = pl.multiple_of((j - t) * blk, blk)
        z = _dot_nt(qm_ref[...], k_ref[pl.ds(start, blk), :])
        sp = _softplus(z)
        log_sig = z - sp
        if diag:
            sp = jnp.where(causal, sp, 0.0)
        hi, lo = _split_bf16(sp)
        tail = _dot(jnp.concatenate([hi, lo], axis=1), upper2)
        if carry is not None:
            tail = tail + jnp.concatenate([carry, carry], axis=1)
        w = jnp.exp(log_sig - tail)
        if diag:
            w = jnp.where(causal, w, 0.0)
        w = w.astype(BF16)
        rowsum = jnp.broadcast_to(jnp.sum(sp, axis=-1, keepdims=True), (rows, half))
        pv = []
        for p in range(n_heads // 2):
            r = _dot(w[2 * p * blk:(2 * p + 2) * blk],
                     v_ref[pl.ds(start, blk), p * half:(p + 1) * half])
            pv.append(jnp.where(lane_h < HEAD_DIM, r[:blk], r[blk:]))
        return pv, rowsum if carry is None else carry + rowsum

    def first_tiles(with_next):
        pv, carry = tile(0, None, True)
        if with_next:
            pv_b, carry = tile(1, carry, False)
            pv = [a + b for a, b in zip(pv, pv_b)]
        carry_ref[...] = carry
        for p in range(n_heads // 2):
            acc_ref[p] = pv[p]
        cmin_ref[0] = jnp.min(carry)

    pl.when(j == 0)(functools.partial(first_tiles, False))
    pl.when(j > 0)(functools.partial(first_tiles, True))

    def live(cmin):
        return cmin <= EXP_UNDERFLOW

    def pair_body(state):
        i, _ = state
        pv_a, carry = tile(2 * i + 2, carry_ref[...], False)
        pv_b, carry = tile(2 * i + 3, carry, False)
        carry_ref[...] = carry
        for p in range(n_heads // 2):
            acc_ref[p] += pv_a[p] + pv_b[p]
        return i + 1, jnp.min(carry)

    _, cmin = lax.while_loop(lambda s: jnp.logical_and(s[0] < (j - 1) // 2, live(s[1])),
                             pair_body, (0, cmin_ref[0]))

    @pl.when(jnp.logical_and(jnp.logical_and(j >= 2, j % 2 == 0), live(cmin)))
    def _():
        pv, _ = tile(j, carry_ref[...], False)
        for p in range(n_heads // 2):
            acc_ref[p] += pv[p]

    y = jnp.concatenate([acc_ref[p] for p in range(n_heads // 2)], axis=-1)
    o_ref[...] = _rms(y, gain_ref[...]).astype(o_ref.dtype)


def _sb_attn(pd, upper2, gain, batch, seq):
    m = pd.shape[0]
    gw = GROUP_WIDTH
    blk = SB_BLOCK
    nq = seq // blk
    n_heads = gw // HEAD_DIM
    whole = pl.BlockSpec(memory_space=pltpu.VMEM)
    return pl.pallas_call(
        _sb_attn_kernel,
        grid=(batch, nq),
        in_specs=[
            pl.BlockSpec((blk, gw), lambda b, j: (b * nq + j, 0)),
            pl.BlockSpec((seq, gw), lambda b, j: (b, 1)),
            pl.BlockSpec((seq, gw), lambda b, j: (b, 2)),
            whole, whole,
        ],
        out_specs=pl.BlockSpec((blk, gw), lambda b, j: (b * nq + j, 0)),
        out_shape=jax.ShapeDtypeStruct((m, gw), BF16),
        scratch_shapes=[pltpu.VMEM((n_heads * blk, gw), BF16),
                        pltpu.VMEM((n_heads // 2, blk, gw // 2), F32),
                        pltpu.VMEM((n_heads * blk, gw // 2), F32),
                        pltpu.SMEM((1,), F32)],
        compiler_params=pltpu.CompilerParams(
            dimension_semantics=("arbitrary", "arbitrary"), vmem_limit_bytes=VMEM_LIMIT_MIX),
        name="sb_attn",
    )(pd, pd, pd, upper2, gain)


def _out_ffn_kernel(x_ref, yabc_ref, yd_ref, wout_ref, g_ref, wgu_ref, wdown_ref, gfin_ref,
                    o_ref, x2_ref, act_ref, *, final_norm):
    n_abc = yabc_ref.shape[1]
    d_ff = wdown_ref.shape[0]
    x2_ref[...] = (x_ref[...]
                   + _dot(yabc_ref[...], wout_ref[0:n_abc, :])
                   + _dot(yd_ref[...], wout_ref[n_abc:, :]))
    h = _rms(x2_ref[...], g_ref[...]).astype(BF16)
    for c in range(d_ff // FF_CHUNK):
        c0 = c * FF_CHUNK
        gate = _dot(h, wgu_ref[:, c0:c0 + FF_CHUNK])
        up = _dot(h, wgu_ref[:, d_ff + c0:d_ff + c0 + FF_CHUNK])
        act_ref[:, c0:c0 + FF_CHUNK] = (gate * (1.0 / (1.0 + jnp.exp(-gate))) * up).astype(BF16)
    out = x2_ref[...] + _dot(act_ref[...], wdown_ref[...])
    if final_norm:
        out = _rms(out, gfin_ref[...])
    o_ref[...] = out


def _out_ffn(x2d, yabc, yd, w_out, g, w_gu, w_down, g_final, final_norm):
    m, d = x2d.shape
    tm = DENSE_ROWS
    whole = pl.BlockSpec(memory_space=pltpu.VMEM)
    vec = pl.BlockSpec((1, d), lambda i: (0, 0))
    return pl.pallas_call(
        functools.partial(_out_ffn_kernel, final_norm=final_norm),
        grid=(m // tm,),
        in_specs=[
            pl.BlockSpec((tm, d), lambda i: (i, 0)),
            pl.BlockSpec((tm, yabc.shape[1]), lambda i: (i, 0)),
            pl.BlockSpec((tm, yd.shape[1]), lambda i: (i, 0)),
            whole, vec, whole, whole, vec,
        ],
        out_specs=pl.BlockSpec((tm, d), lambda i: (i, 0)),
        out_shape=jax.ShapeDtypeStruct((m, d), F32),
        scratch_shapes=[pltpu.VMEM((tm, d), F32), pltpu.VMEM((tm, w_down.shape[0]), BF16)],
        compiler_params=pltpu.CompilerParams(
            dimension_semantics=("arbitrary",), vmem_limit_bytes=VMEM_LIMIT_DENSE),
        name="out_ffn",
    )(x2d, yabc, yd, w_out, g, w_gu, w_down, g_final)


def _t5_bucket(dist):
    max_exact = N_BUCKETS // 2
    df = jnp.maximum(dist, 1).astype(F32)
    large = max_exact + (jnp.log(df / max_exact) / math.log(MAX_DISTANCE / max_exact)
                         * (N_BUCKETS - max_exact)).astype(jnp.int32)
    large = jnp.minimum(large, N_BUCKETS - 1)
    return jnp.where(dist < max_exact, dist, large)


def kernel(x, w_in, w_out, sgu_w, sgu_b, pool_w, pool_scale, swa_sinks, rel_bias, mix_out_gain,
           norm_mix, norm_ffn, w_gate_up, w_down, norm_final):
    batch, seq, d = x.shape
    depth = w_in.shape[0]
    gw = GROUP_WIDTH
    n_heads = gw // HEAD_DIM
    kv_heads = (w_in.shape[2] - 7 * gw) // (2 * HEAD_DIM)
    rep = n_heads // kv_heads
    assert seq % LOCAL_ROWS == 0 and seq % SB_BLOCK == 0 and (batch * seq) % DENSE_ROWS == 0

    head_id = np.arange(gw) // HEAD_DIM
    avg = (head_id[:, None] == head_id[None, :]) / HEAD_DIM
    avg2 = jnp.asarray(np.concatenate([avg, avg], axis=0), BF16)
    upper = np.arange(SB_BLOCK)[:, None] > np.arange(SB_BLOCK)[None, :]
    upper2 = jnp.asarray(np.concatenate([upper, upper], axis=0), BF16)
    dist = (jnp.arange(CHUNK)[:, None] + CHUNK) - jnp.arange(2 * CHUNK)[None, :]
    bucket = _t5_bucket(jnp.clip(dist, 0, CHUNK - 1)).astype(jnp.int32)
    pool_group = gw // len(POOL_WINDOWS)
    scale = HEAD_DIM ** -0.5

    kvw = kv_heads * HEAD_DIM
    off = np.cumsum([0, gw, gw, gw, gw, kvw, kvw, gw, gw])

    def per_query_head(w, start):
        return [w[:, start + (hq // rep) * HEAD_DIM:start + (hq // rep + 1) * HEAD_DIM]
                for hq in range(n_heads)]

    x2d = x.reshape(batch * seq, d)
    for l in range(depth):
        w = w_in[l]
        w_l = jnp.concatenate(
            [w[:, :off[3]], w[:, off[3]:off[4]] * scale]
            + per_query_head(w, off[4]) + per_query_head(w, off[5])
            + [w[:, off[6]:off[7]] * scale, w[:, off[7]:]], axis=1).astype(BF16)
        pa, pc, pd = _norm_proj(x2d, norm_mix[l].reshape(1, d), w_l, 3 * gw, 3 * gw, 3 * gw)

        pool_bd = jnp.zeros((gw, gw), F32)
        for g in range(len(POOL_WINDOWS)):
            pool_bd = lax.dynamic_update_slice(pool_bd, pool_w[l, g], (g * pool_group, g * pool_group))
        sgu_b_tile = jnp.repeat(sgu_b[l].T, HEAD_DIM, axis=1)
        gain = mix_out_gain[l].reshape(1, 4 * gw)
        yabc = _local_mix(pa, pc, rel_bias, swa_sinks[l], sgu_w[l], sgu_b_tile, avg2,
                          pool_bd.astype(BF16), pool_scale[l].reshape(1, gw), bucket,
                          gain[:, :3 * gw], batch, seq)
        yd = _sb_attn(pd, upper2, gain[:, 3 * gw:], batch, seq)

        x2d = _out_ffn(x2d, yabc, yd, w_out[l].astype(BF16), norm_ffn[l].reshape(1, d),
                       w_gate_up[l].astype(BF16), w_down[l].astype(BF16),
                       norm_final.reshape(1, d), l == depth - 1)
    return x2d.reshape(batch, seq, d)
```

```python
import functools
import math

import numpy as np
import jax
import jax.numpy as jnp
from jax import lax
from jax.experimental import pallas as pl
from jax.experimental.pallas import tpu as pltpu

F32 = jnp.float32
BF16 = jnp.bfloat16

HEAD_DIM = 64
GROUP_WIDTH = 256
CHUNK = 128
POOL_WINDOWS = (2, 4, 8, 16)
POOL_HALO = 16
N_BUCKETS = 32
MAX_DISTANCE = 128
EPS = 1e-6
MASK_VALUE = -1e30
EXP_UNDERFLOW = 104.0

LOCAL_ROWS = 512
SB_BLOCK = 256
DENSE_ROWS = 512
FF_CHUNK = 256
VMEM_LIMIT_DENSE = 52 * 1024 * 1024
VMEM_LIMIT_MIX = 40 * 1024 * 1024


def _rms(x, g):
    return x * lax.rsqrt(jnp.mean(x * x, axis=-1, keepdims=True) + EPS) * g


def _gelu_tanh(x):
    inner = math.sqrt(2.0 / math.pi) * (x + 0.044715 * (x * x * x))
    return x * (0.5 * (1.0 + jnp.tanh(inner)))


def _split_bf16(x):
    hi = x.astype(BF16)
    lo = (x - hi.astype(F32)).astype(BF16)
    return hi, lo


def _dot(a, b):
    return jnp.dot(a, b, preferred_element_type=F32)


def _dot_nt(a, b):
    return lax.dot_general(a, b, (((1,), (1,)), ((), ())), preferred_element_type=F32)


def _per_query_head(kv):
    lane = lax.broadcasted_iota(jnp.int32, (1, kv.shape[1]), 1)
    swapped = pltpu.roll(kv, HEAD_DIM, 1)
    first = jnp.where(lane < HEAD_DIM, kv, swapped)
    second = jnp.where(lane < HEAD_DIM, swapped, kv)
    return jnp.concatenate([first, second], axis=1)


def _norm_proj_kernel(x_ref, g_ref, w_ref, pa_ref, pc_ref, pd_ref):
    gw = GROUP_WIDTH
    kvw = w_ref.shape[1] - 7 * gw
    scale = HEAD_DIM ** -0.5
    h = _rms(x_ref[...], g_ref[...]).astype(BF16)
    y = _dot(h, w_ref[...])
    pa_ref[...] = y[:, :3 * gw]
    c_k = 4 * gw
    c_v = c_k + kvw // 2
    d_q = c_v + kvw // 2
    pc_ref[:, 0:gw] = (y[:, 3 * gw:4 * gw] * scale).astype(BF16)
    pc_ref[:, gw:2 * gw] = _per_query_head(y[:, c_k:c_v]).astype(BF16)
    pc_ref[:, 2 * gw:3 * gw] = _per_query_head(y[:, c_v:d_q]).astype(BF16)
    pd_ref[:, 0:gw] = (y[:, d_q:d_q + gw] * scale).astype(BF16)
    pd_ref[:, gw:3 * gw] = y[:, d_q + gw:].astype(BF16)


def _norm_proj(x2d, g, w):
    m, d = x2d.shape
    tm = DENSE_ROWS
    na = nc = nd = 3 * GROUP_WIDTH
    assert w.shape[1] == 7 * GROUP_WIDTH + 4 * HEAD_DIM, "expects 2 kv heads of HEAD_DIM"
    return pl.pallas_call(
        _norm_proj_kernel,
        grid=(m // tm,),
        in_specs=[
            pl.BlockSpec((tm, d), lambda i: (i, 0)),
            pl.BlockSpec((1, d), lambda i: (0, 0)),
            pl.BlockSpec(memory_space=pltpu.VMEM),
        ],
        out_specs=[
            pl.BlockSpec((tm, na), lambda i: (i, 0)),
            pl.BlockSpec((tm, nc), lambda i: (i, 0)),
            pl.BlockSpec((tm, nd), lambda i: (i, 0)),
        ],
        out_shape=[
            jax.ShapeDtypeStruct((m, na), F32),
            jax.ShapeDtypeStruct((m, nc), BF16),
            jax.ShapeDtypeStruct((m, nd), BF16),
        ],
        compiler_params=pltpu.CompilerParams(
            dimension_semantics=("arbitrary",), vmem_limit_bytes=VMEM_LIMIT_MIX),
        name="norm_proj",
    )(x2d, g, w)


def _mask_heads(x, n_heads):
    lane = lax.broadcasted_iota(jnp.int32, (1, x.shape[1]), 1)
    return jnp.concatenate(
        [jnp.where(lane // HEAD_DIM == h, x, jnp.zeros_like(x)) for h in range(n_heads)], axis=0)


def _local_mix_kernel(relb_ref, sink_ref,
                      pa_ref, halo_ref, pc_ref, kh_ref, vh_ref,
                      sguw_ref, sgub_ref, avg_ref, poolw_ref, pscale_ref,
                      bucket_ref, gain_ref,
                      o_ref,
                      bias_ref, wcat_ref):
    b = pl.program_id(0)
    j = pl.program_id(1)
    rows = pa_ref.shape[0]
    gw = GROUP_WIDTH
    n_heads = gw // HEAD_DIM
    n_chunks = rows // CHUNK
    lane = lax.broadcasted_iota(jnp.int32, (1, gw), 1)

    @pl.when(jnp.logical_and(b == 0, j == 0))
    def _():
        bucket = bucket_ref[...]
        tril = (lax.broadcasted_iota(jnp.int32, (CHUNK, CHUNK), 1)
                <= lax.broadcasted_iota(jnp.int32, (CHUNK, CHUNK), 0))
        for h in range(n_heads):
            acc = jnp.zeros(bucket.shape, F32)
            for bk in range(N_BUCKETS):
                acc = jnp.where(bucket == bk, relb_ref[bk, h], acc)
            bias_ref[h * CHUNK:(h + 1) * CHUNK, :] = acc
            wcat_ref[:, h * CHUNK:(h + 1) * CHUNK] = jnp.where(tril, sguw_ref[h], 0.0).astype(BF16)

    gain = gain_ref[...]

    p = pa_ref[:, 2 * gw:3 * gw]
    halo = jnp.where(j == 0, 0.0, halo_ref[...])
    ext = jnp.concatenate([halo, p], axis=0)
    sums = []
    run = ext
    for shift in (1, 2, 4, 8):
        run = run + pltpu.roll(run, shift, 0)
        sums.append(run[POOL_HALO:])
    group = lane // (gw // len(POOL_WINDOWS))
    wsum = jnp.where(group == 0, sums[0],
                     jnp.where(group == 1, sums[1],
                               jnp.where(group == 2, sums[2], sums[3])))
    win = jnp.where(group == 0, POOL_WINDOWS[0],
                    jnp.where(group == 1, POOL_WINDOWS[1],
                              jnp.where(group == 2, POOL_WINDOWS[2], POOL_WINDOWS[3])))
    t_seq = j * rows + lax.broadcasted_iota(jnp.int32, (rows, 1), 0)
    count = jnp.minimum(t_seq + 1, win).astype(F32)
    yb = wsum / count - p
    yb = _dot(yb.astype(BF16), poolw_ref[...]) * pscale_ref[...]
    o_ref[:, gw:2 * gw] = _rms(yb, gain[:, gw:2 * gw]).astype(o_ref.dtype)

    avg = avg_ref[...]
    u = _gelu_tanh(pa_ref[:, 0:gw])
    v = _gelu_tanh(pa_ref[:, gw:2 * gw])
    v_hi, v_lo = _split_bf16(v)
    vc = v - _dot(jnp.concatenate([v_hi, v_lo], axis=1), avg)
    sq_hi, sq_lo = _split_bf16(vc * vc)
    var = _dot(jnp.concatenate([sq_hi, sq_lo], axis=1), avg)
    vn = (vc * lax.rsqrt(var + EPS)).astype(BF16)
    rhs = jnp.concatenate(
        [_mask_heads(vn[c * CHUNK:(c + 1) * CHUNK], n_heads) for c in range(n_chunks)], axis=1)
    mix = _dot(wcat_ref[...], rhs)
    for c in range(n_chunks):
        r0 = c * CHUNK
        ya = u[r0:r0 + CHUNK] * (mix[:, c * gw:(c + 1) * gw] + sgub_ref[...])
        o_ref[r0:r0 + CHUNK, 0:gw] = _rms(ya, gain[:, 0:gw]).astype(o_ref.dtype)

    stacked = (n_heads * CHUNK, 2 * CHUNK)
    qi = lax.broadcasted_iota(jnp.int32, stacked, 0) & (CHUNK - 1)
    ki = lax.broadcasted_iota(jnp.int32, stacked, 1)
    in_window = jnp.logical_and(ki > qi, ki <= qi + CHUNK)
    head_of_row = lax.broadcasted_iota(jnp.int32, (n_heads * CHUNK, 1), 0) // CHUNK
    sink = jnp.zeros((n_heads * CHUNK, 1), F32)
    for h in range(n_heads):
        sink = jnp.where(head_of_row == h, sink_ref[h], sink)
    bias = bias_ref[...]
    for c in range(n_chunks):
        r0 = c * CHUNK
        if c == 0:
            k_prev, v_prev = kh_ref[...], vh_ref[...]
            mask = jnp.logical_and(in_window, jnp.logical_or(ki >= CHUNK, j > 0))
        else:
            k_prev = pc_ref[r0 - CHUNK:r0, gw:2 * gw]
            v_prev = pc_ref[r0 - CHUNK:r0, 2 * gw:3 * gw]
            mask = in_window
        k2 = jnp.concatenate([k_prev, pc_ref[r0:r0 + CHUNK, gw:2 * gw]], axis=0)
        v2 = jnp.concatenate([v_prev, pc_ref[r0:r0 + CHUNK, 2 * gw:3 * gw]], axis=0)
        qm = _mask_heads(pc_ref[r0:r0 + CHUNK, 0:gw], n_heads)
        logits = jnp.where(mask, _dot_nt(qm, k2) + bias, MASK_VALUE)
        mx = jnp.maximum(jnp.max(logits, axis=-1, keepdims=True), sink)
        e = jnp.exp(logits - mx)
        denom = jnp.sum(e, axis=-1, keepdims=True) + jnp.exp(sink - mx)
        probs = (e * (1.0 / denom)).astype(BF16)
        pcat = jnp.concatenate([probs[h * CHUNK:(h + 1) * CHUNK] for h in range(n_heads)], axis=1)
        yc = _dot(pcat, _mask_heads(v2, n_heads))
        o_ref[r0:r0 + CHUNK, 2 * gw:3 * gw] = _rms(yc, gain[:, 2 * gw:3 * gw]).astype(o_ref.dtype)


def _local_mix(pa, pc, rel_bias, sinks, sgu_w, sgu_b_tile, avg, pool_bd, pool_scale,
               bucket, gain, batch, seq):
    m = pa.shape[0]
    rows = LOCAL_ROWS
    nq = seq // rows
    gw = GROUP_WIDTH
    n_heads = gw // HEAD_DIM

    def row_blk(b, j):
        return b * nq + j

    halo_per_blk = rows // POOL_HALO
    kv_per_blk = rows // CHUNK
    smem = pl.BlockSpec(memory_space=pltpu.SMEM)
    whole = pl.BlockSpec(memory_space=pltpu.VMEM)
    return pl.pallas_call(
        _local_mix_kernel,
        grid=(batch, nq),
        in_specs=[
            smem, smem,
            pl.BlockSpec((rows, 3 * gw), lambda b, j: (row_blk(b, j), 0)),
            pl.BlockSpec((POOL_HALO, gw),
                         lambda b, j: (jnp.maximum(row_blk(b, j) * halo_per_blk - 1, 0), 2)),
            pl.BlockSpec((rows, 3 * gw), lambda b, j: (row_blk(b, j), 0)),
            pl.BlockSpec((CHUNK, gw),
                         lambda b, j: (jnp.maximum(row_blk(b, j) * kv_per_blk - 1, 0), 1)),
            pl.BlockSpec((CHUNK, gw),
                         lambda b, j: (jnp.maximum(row_blk(b, j) * kv_per_blk - 1, 0), 2)),
            whole, whole, whole, whole, whole, whole, whole,
        ],
        out_specs=pl.BlockSpec((rows, 3 * gw), lambda b, j: (row_blk(b, j), 0)),
        out_shape=jax.ShapeDtypeStruct((m, 3 * gw), BF16),
        scratch_shapes=[pltpu.VMEM((n_heads * CHUNK, 2 * CHUNK), F32),
                        pltpu.VMEM((CHUNK, n_heads * CHUNK), BF16)],
        compiler_params=pltpu.CompilerParams(
            dimension_semantics=("arbitrary", "arbitrary"), vmem_limit_bytes=VMEM_LIMIT_MIX),
        name="local_mix",
    )(rel_bias, sinks, pa, pa, pc, pc, pc, sgu_w, sgu_b_tile, avg, pool_bd, pool_scale,
      bucket, gain)


def _softplus(z):
    neg_abs = lax.bitcast_convert_type(
        lax.bitcast_convert_type(z, jnp.uint32) | jnp.uint32(0x80000000), F32)
    return jnp.maximum(z, 0.0) + jnp.log(1.0 + jnp.exp(neg_abs))


def _sb_attn_kernel(q_ref, k_ref, v_ref, upper_ref, gain_ref, o_ref,
                    qm_ref, acc_ref, carry_ref, cmin_ref):
    j = pl.program_id(1)
    blk, gw = q_ref.shape
    half = gw // 2
    n_heads = gw // HEAD_DIM
    rows = n_heads * blk
    lane = lax.broadcasted_iota(jnp.int32, (1, gw), 1)
    lane_h = lax.broadcasted_iota(jnp.int32, (1, half), 1)

    q = q_ref[...]
    for h in range(n_heads):
        qm_ref[h * blk:(h + 1) * blk, :] = jnp.where(lane // HEAD_DIM == h, q, jnp.zeros_like(q))
    upper = upper_ref[...]
    row = lax.broadcasted_iota(jnp.int32, (rows, blk), 0) & (blk - 1)
    col = lax.broadcasted_iota(jnp.int32, (rows, blk), 1)
    causal = col < row

    def tile(t, carry, diag):
        start = pl.multiple_of((j - t) * blk, blk)
        z = _dot_nt(qm_ref[...], k_ref[pl.ds(start, blk), :])
        sp = _softplus(z)
        log_sig = z - sp
        if diag:
            sp = jnp.where(causal, sp, 0.0)
        tail = _dot(sp.astype(BF16), upper)
        if carry is not None:
            tail = tail + jnp.concatenate([carry, carry], axis=1)
        w = jnp.exp(log_sig - tail)
        if diag:
            w = jnp.where(causal, w, 0.0)
        w = w.astype(BF16)
        rowsum = jnp.broadcast_to(jnp.sum(sp, axis=-1, keepdims=True), (rows, half))
        pv = []
        for p in range(n_heads // 2):
            r = _dot(w[2 * p * blk:(2 * p + 2) * blk],
                     v_ref[pl.ds(start, blk), p * half:(p + 1) * half])
            pv.append(jnp.where(lane_h < HEAD_DIM, r[:blk], r[blk:]))
        return pv, rowsum if carry is None else carry + rowsum

    def first_tiles(with_next):
        pv, carry = tile(0, None, True)
        if with_next:
            pv_b, carry = tile(1, carry, False)
            pv = [a + b for a, b in zip(pv, pv_b)]
        carry_ref[...] = carry
        for p in range(n_heads // 2):
            acc_ref[p] = pv[p]
        cmin_ref[0] = jnp.min(carry)

    pl.when(j == 0)(functools.partial(first_tiles, False))
    pl.when(j > 0)(functools.partial(first_tiles, True))

    def live(cmin):
        return cmin <= EXP_UNDERFLOW

    def pair_body(state):
        i, _ = state
        pv_a, carry = tile(2 * i + 2, carry_ref[...], False)
        pv_b, carry = tile(2 * i + 3, carry, False)
        carry_ref[...] = carry
        for p in range(n_heads // 2):
            acc_ref[p] += pv_a[p] + pv_b[p]
        return i + 1, jnp.min(carry)

    _, cmin = lax.while_loop(lambda s: jnp.logical_and(s[0] < (j - 1) // 2, live(s[1])),
                             pair_body, (0, cmin_ref[0]))

    @pl.when(jnp.logical_and(jnp.logical_and(j >= 2, j % 2 == 0), live(cmin)))
    def _():
        pv, _ = tile(j, carry_ref[...], False)
        for p in range(n_heads // 2):
            acc_ref[p] += pv[p]

    y = jnp.concatenate([acc_ref[p] for p in range(n_heads // 2)], axis=-1)
    o_ref[...] = _rms(y, gain_ref[...]).astype(o_ref.dtype)


def _sb_attn(pd, upper, gain, batch, seq):
    m = pd.shape[0]
    gw = GROUP_WIDTH
    blk = SB_BLOCK
    nq = seq // blk
    n_heads = gw // HEAD_DIM
    whole = pl.BlockSpec(memory_space=pltpu.VMEM)
    return pl.pallas_call(
        _sb_attn_kernel,
        grid=(batch, nq),
        in_specs=[
            pl.BlockSpec((blk, gw), lambda b, j: (b * nq + j, 0)),
            pl.BlockSpec((seq, gw), lambda b, j: (b, 1)),
            pl.BlockSpec((seq, gw), lambda b, j: (b, 2)),
            whole, whole,
        ],
        out_specs=pl.BlockSpec((blk, gw), lambda b, j: (b * nq + j, 0)),
        out_shape=jax.ShapeDtypeStruct((m, gw), BF16),
        scratch_shapes=[pltpu.VMEM((n_heads * blk, gw), BF16),
                        pltpu.VMEM((n_heads // 2, blk, gw // 2), F32),
                        pltpu.VMEM((n_heads * blk, gw // 2), F32),
                        pltpu.SMEM((1,), F32)],
        compiler_params=pltpu.CompilerParams(
            dimension_semantics=("arbitrary", "arbitrary"), vmem_limit_bytes=VMEM_LIMIT_MIX),
        name="sb_attn",
    )(pd, pd, pd, upper, gain)


def _out_ffn_kernel(x_ref, yabc_ref, yd_ref, wout_ref, g_ref, wgu_ref, wdown_ref, gfin_ref,
                    o_ref, x2_ref, act_ref, *, final_norm):
    n_abc = yabc_ref.shape[1]
    d_ff = wdown_ref.shape[0]
    x2_ref[...] = (x_ref[...]
                   + _dot(yabc_ref[...], wout_ref[0:n_abc, :])
                   + _dot(yd_ref[...], wout_ref[n_abc:, :]))
    h = _rms(x2_ref[...], g_ref[...]).astype(BF16)
    for c in range(d_ff // FF_CHUNK):
        c0 = c * FF_CHUNK
        gate = _dot(h, wgu_ref[:, c0:c0 + FF_CHUNK])
        up = _dot(h, wgu_ref[:, d_ff + c0:d_ff + c0 + FF_CHUNK])
        act_ref[:, c0:c0 + FF_CHUNK] = (gate * (1.0 / (1.0 + jnp.exp(-gate))) * up).astype(BF16)
    out = x2_ref[...] + _dot(act_ref[...], wdown_ref[...])
    if final_norm:
        out = _rms(out, gfin_ref[...])
    o_ref[...] = out


def _out_ffn(x2d, yabc, yd, w_out, g, w_gu, w_down, g_final, final_norm):
    m, d = x2d.shape
    tm = DENSE_ROWS
    whole = pl.BlockSpec(memory_space=pltpu.VMEM)
    vec = pl.BlockSpec((1, d), lambda i: (0, 0))
    return pl.pallas_call(
        functools.partial(_out_ffn_kernel, final_norm=final_norm),
        grid=(m // tm,),
        in_specs=[
            pl.BlockSpec((tm, d), lambda i: (i, 0)),
            pl.BlockSpec((tm, yabc.shape[1]), lambda i: (i, 0)),
            pl.BlockSpec((tm, yd.shape[1]), lambda i: (i, 0)),
            whole, vec, whole, whole, vec,
        ],
        out_specs=pl.BlockSpec((tm, d), lambda i: (i, 0)),
        out_shape=jax.ShapeDtypeStruct((m, d), F32),
        scratch_shapes=[pltpu.VMEM((tm, d), F32), pltpu.VMEM((tm, w_down.shape[0]), BF16)],
        compiler_params=pltpu.CompilerParams(
            dimension_semantics=("arbitrary",), vmem_limit_bytes=VMEM_LIMIT_DENSE),
        name="out_ffn",
    )(x2d, yabc, yd, w_out, g, w_gu, w_down, g_final)


def _t5_bucket(dist):
    max_exact = N_BUCKETS // 2
    df = jnp.maximum(dist, 1).astype(F32)
    large = max_exact + (jnp.log(df / max_exact) / math.log(MAX_DISTANCE / max_exact)
                         * (N_BUCKETS - max_exact)).astype(jnp.int32)
    large = jnp.minimum(large, N_BUCKETS - 1)
    return jnp.where(dist < max_exact, dist, large)


def kernel(x, w_in, w_out, sgu_w, sgu_b, pool_w, pool_scale, swa_sinks, rel_bias, mix_out_gain,
           norm_mix, norm_ffn, w_gate_up, w_down, norm_final):
    batch, seq, d = x.shape
    depth = w_in.shape[0]
    gw = GROUP_WIDTH
    n_pool = len(POOL_WINDOWS)
    assert seq % LOCAL_ROWS == 0 and seq % SB_BLOCK == 0 and (batch * seq) % DENSE_ROWS == 0

    head_id = np.arange(gw) // HEAD_DIM
    avg = (head_id[:, None] == head_id[None, :]) / HEAD_DIM
    avg2 = jnp.asarray(np.concatenate([avg, avg], axis=0), BF16)
    upper = jnp.asarray(np.arange(SB_BLOCK)[:, None] > np.arange(SB_BLOCK)[None, :], BF16)
    dist = (jnp.arange(CHUNK)[:, None] + CHUNK) - jnp.arange(2 * CHUNK)[None, :]
    bucket = _t5_bucket(jnp.clip(dist, 0, CHUNK - 1)).astype(jnp.int32)

    w_in_b, w_out_b = w_in.astype(BF16), w_out.astype(BF16)
    w_gu_b, w_down_b = w_gate_up.astype(BF16), w_down.astype(BF16)
    pool_bd = (pool_w[:, :, :, None, :] * jnp.eye(n_pool, dtype=F32)[None, :, None, :, None])
    pool_bd = pool_bd.reshape(depth, gw, gw).astype(BF16)
    sgu_b_tile = jnp.repeat(jnp.swapaxes(sgu_b, 1, 2), HEAD_DIM, axis=2)
    gains = mix_out_gain.reshape(depth, 1, 4 * gw)

    x2d = x.reshape(batch * seq, d)
    for l in range(depth):
        pa, pc, pd = _norm_proj(x2d, norm_mix[l].reshape(1, d), w_in_b[l])
        yabc = _local_mix(pa, pc, rel_bias, swa_sinks[l], sgu_w[l], sgu_b_tile[l], avg2,
                          pool_bd[l], pool_scale[l].reshape(1, gw), bucket,
                          gains[l, :, :3 * gw], batch, seq)
        yd = _sb_attn(pd, upper, gains[l, :, 3 * gw:], batch, seq)

        x2d = _out_ffn(x2d, yabc, yd, w_out_b[l], norm_ffn[l].reshape(1, d),
                       w_gu_b[l], w_down_b[l],
                       norm_final.reshape(1, d), l == depth - 1)
    return x2d.reshape(batch, seq, d)
```

```python
import functools
import math

import numpy as np
import jax
import jax.numpy as jnp
from jax import lax
from jax.experimental import pallas as pl
from jax.experimental.pallas import tpu as pltpu

F32 = jnp.float32
BF16 = jnp.bfloat16

HEAD_DIM = 64
GROUP_WIDTH = 256
CHUNK = 128
POOL_WINDOWS = (2, 4, 8, 16)
POOL_HALO = 16
N_BUCKETS = 32
MAX_DISTANCE = 128
EPS = 1e-6
MASK_VALUE = -1e30
EXP_UNDERFLOW = 104.0

LOCAL_ROWS = 512
SB_BLOCK = 256
DENSE_ROWS = 512
FF_CHUNK = 256
VMEM_LIMIT_DENSE = 52 * 1024 * 1024
VMEM_LIMIT_MIX = 40 * 1024 * 1024


def _rms(x, g):
    return x * lax.rsqrt(jnp.mean(x * x, axis=-1, keepdims=True) + EPS) * g


def _gelu_tanh(x):
    inner = math.sqrt(2.0 / math.pi) * (x + 0.044715 * (x * x * x))
    return x * (0.5 * (1.0 + jnp.tanh(inner)))


def _split_bf16(x):
    hi = x.astype(BF16)
    lo = (x - hi.astype(F32)).astype(BF16)
    return hi, lo


def _dot(a, b):
    return jnp.dot(a, b, preferred_element_type=F32)


def _dot_nt(a, b):
    return lax.dot_general(a, b, (((1,), (1,)), ((), ())), preferred_element_type=F32)


def _per_query_head(kv):
    lane = lax.broadcasted_iota(jnp.int32, (1, kv.shape[1]), 1)
    swapped = pltpu.roll(kv, HEAD_DIM, 1)
    first = jnp.where(lane < HEAD_DIM, kv, swapped)
    second = jnp.where(lane < HEAD_DIM, swapped, kv)
    return jnp.concatenate([first, second], axis=1)


def _norm_proj_kernel(x_ref, g_ref, w_ref, pa_ref, pc_ref, pd_ref):
    gw = GROUP_WIDTH
    kvw = w_ref.shape[1] - 7 * gw
    scale = HEAD_DIM ** -0.5
    h = _rms(x_ref[...], g_ref[...]).astype(BF16)
    y = _dot(h, w_ref[...])
    pa_ref[...] = y[:, :3 * gw]
    c_k = 4 * gw
    c_v = c_k + kvw // 2
    d_q = c_v + kvw // 2
    pc_ref[:, 0:gw] = (y[:, 3 * gw:4 * gw] * scale).astype(BF16)
    pc_ref[:, gw:2 * gw] = _per_query_head(y[:, c_k:c_v]).astype(BF16)
    pc_ref[:, 2 * gw:3 * gw] = _per_query_head(y[:, c_v:d_q]).astype(BF16)
    pd_ref[:, 0:gw] = (y[:, d_q:d_q + gw] * scale).astype(BF16)
    pd_ref[:, gw:3 * gw] = y[:, d_q + gw:].astype(BF16)


def _norm_proj(x2d, g, w):
    m, d = x2d.shape
    tm = DENSE_ROWS
    na = nc = nd = 3 * GROUP_WIDTH
    assert w.shape[1] == 7 * GROUP_WIDTH + 4 * HEAD_DIM, "expects 2 kv heads of HEAD_DIM"
    return pl.pallas_call(
        _norm_proj_kernel,
        grid=(m // tm,),
        in_specs=[
            pl.BlockSpec((tm, d), lambda i: (i, 0)),
            pl.BlockSpec((1, d), lambda i: (0, 0)),
            pl.BlockSpec(memory_space=pltpu.VMEM),
        ],
        out_specs=[
            pl.BlockSpec((tm, na), lambda i: (i, 0)),
            pl.BlockSpec((tm, nc), lambda i: (i, 0)),
            pl.BlockSpec((tm, nd), lambda i: (i, 0)),
        ],
        out_shape=[
            jax.ShapeDtypeStruct((m, na), F32),
            jax.ShapeDtypeStruct((m, nc), BF16),
            jax.ShapeDtypeStruct((m, nd), BF16),
        ],
        compiler_params=pltpu.CompilerParams(
            dimension_semantics=("arbitrary",), vmem_limit_bytes=VMEM_LIMIT_MIX),
        name="norm_proj",
    )(x2d, g, w)


def _mask_heads(x, n_heads):
    lane = lax.broadcasted_iota(jnp.int32, (1, x.shape[1]), 1)
    return jnp.concatenate(
        [jnp.where(lane // HEAD_DIM == h, x, jnp.zeros_like(x)) for h in range(n_heads)], axis=0)


def _local_mix_tables(relb_ref, sguw_ref, bucket_ref, bias_ref, wcat_ref):
    bucket = bucket_ref[...]
    tril = (lax.broadcasted_iota(jnp.int32, (CHUNK, CHUNK), 1)
            <= lax.broadcasted_iota(jnp.int32, (CHUNK, CHUNK), 0))
    for h in range(sguw_ref.shape[0]):
        acc = jnp.zeros(bucket.shape, F32)
        for bk in range(N_BUCKETS):
            acc = jnp.where(bucket == bk, relb_ref[bk, h], acc)
        bias_ref[h * CHUNK:(h + 1) * CHUNK, :] = acc
        wcat_ref[:, h * CHUNK:(h + 1) * CHUNK] = jnp.where(tril, sguw_ref[h], 0.0).astype(BF16)


def _local_mix_body(j, sink_ref,
                    pa_ref, halo_ref, pc_ref, kh_ref, vh_ref,
                    sgub_ref, avg_ref, poolw_ref, pscale_ref, gain_ref,
                    o_ref,
                    bias_ref, wcat_ref):
    rows = pa_ref.shape[0]
    gw = GROUP_WIDTH
    n_heads = gw // HEAD_DIM
    n_chunks = rows // CHUNK
    lane = lax.broadcasted_iota(jnp.int32, (1, gw), 1)
    gain = gain_ref[...]

    p = pa_ref[:, 2 * gw:3 * gw]
    halo = jnp.where(j == 0, 0.0, halo_ref[...])
    ext = jnp.concatenate([halo, p], axis=0)
    sums = []
    run = ext
    for shift in (1, 2, 4, 8):
        run = run + pltpu.roll(run, shift, 0)
        sums.append(run[POOL_HALO:])
    group = lane // (gw // len(POOL_WINDOWS))
    wsum = jnp.where(group == 0, sums[0],
                     jnp.where(group == 1, sums[1],
                               jnp.where(group == 2, sums[2], sums[3])))
    win = jnp.where(group == 0, POOL_WINDOWS[0],
                    jnp.where(group == 1, POOL_WINDOWS[1],
                              jnp.where(group == 2, POOL_WINDOWS[2], POOL_WINDOWS[3])))
    t_seq = j * rows + lax.broadcasted_iota(jnp.int32, (rows, 1), 0)
    count = jnp.minimum(t_seq + 1, win).astype(F32)
    yb = wsum / count - p
    yb = _dot(yb.astype(BF16), poolw_ref[...]) * pscale_ref[...]
    o_ref[:, gw:2 * gw] = _rms(yb, gain[:, gw:2 * gw]).astype(o_ref.dtype)
    yield

    avg = avg_ref[...]
    u = _gelu_tanh(pa_ref[:, 0:gw])
    v = _gelu_tanh(pa_ref[:, gw:2 * gw])
    v_hi, v_lo = _split_bf16(v)
    vc = v - _dot(jnp.concatenate([v_hi, v_lo], axis=1), avg)
    sq_hi, sq_lo = _split_bf16(vc * vc)
    var = _dot(jnp.concatenate([sq_hi, sq_lo], axis=1), avg)
    vn = (vc * lax.rsqrt(var + EPS)).astype(BF16)
    rhs = jnp.concatenate(
        [_mask_heads(vn[c * CHUNK:(c + 1) * CHUNK], n_heads) for c in range(n_chunks)], axis=1)
    mix = _dot(wcat_ref[...], rhs)
    for c in range(n_chunks):
        r0 = c * CHUNK
        ya = u[r0:r0 + CHUNK] * (mix[:, c * gw:(c + 1) * gw] + sgub_ref[...])
        o_ref[r0:r0 + CHUNK, 0:gw] = _rms(ya, gain[:, 0:gw]).astype(o_ref.dtype)
    yield

    stacked = (n_heads * CHUNK, 2 * CHUNK)
    qi = lax.broadcasted_iota(jnp.int32, stacked, 0) & (CHUNK - 1)
    ki = lax.broadcasted_iota(jnp.int32, stacked, 1)
    in_window = jnp.logical_and(ki > qi, ki <= qi + CHUNK)
    head_of_row = lax.broadcasted_iota(jnp.int32, (n_heads * CHUNK, 1), 0) // CHUNK
    sink = jnp.zeros((n_heads * CHUNK, 1), F32)
    for h in range(n_heads):
        sink = jnp.where(head_of_row == h, sink_ref[h], sink)
    bias = bias_ref[...]
    for c in range(n_chunks):
        r0 = c * CHUNK
        if c == 0:
            k_prev, v_prev = kh_ref[...], vh_ref[...]
            mask = jnp.logical_and(in_window, jnp.logical_or(ki >= CHUNK, j > 0))
        else:
            k_prev = pc_ref[r0 - CHUNK:r0, gw:2 * gw]
            v_prev = pc_ref[r0 - CHUNK:r0, 2 * gw:3 * gw]
            mask = in_window
        k2 = jnp.concatenate([k_prev, pc_ref[r0:r0 + CHUNK, gw:2 * gw]], axis=0)
        v2 = jnp.concatenate([v_prev, pc_ref[r0:r0 + CHUNK, 2 * gw:3 * gw]], axis=0)
        qm = _mask_heads(pc_ref[r0:r0 + CHUNK, 0:gw], n_heads)
        logits = jnp.where(mask, _dot_nt(qm, k2) + bias, MASK_VALUE)
        mx = jnp.maximum(jnp.max(logits, axis=-1, keepdims=True), sink)
        e = jnp.exp(logits - mx)
        denom = jnp.sum(e, axis=-1, keepdims=True) + jnp.exp(sink - mx)
        probs = (e * (1.0 / denom)).astype(BF16)
        pcat = jnp.concatenate([probs[h * CHUNK:(h + 1) * CHUNK] for h in range(n_heads)], axis=1)
        yc = _dot(pcat, _mask_heads(v2, n_heads))
        o_ref[r0:r0 + CHUNK, 2 * gw:3 * gw] = _rms(yc, gain[:, 2 * gw:3 * gw]).astype(o_ref.dtype)
        yield


def _softplus(z):
    neg_abs = lax.bitcast_convert_type(
        lax.bitcast_convert_type(z, jnp.uint32) | jnp.uint32(0x80000000), F32)
    return jnp.maximum(z, 0.0) + jnp.log(1.0 + jnp.exp(neg_abs))


def _sb_attn_kernel(q_ref, k_ref, v_ref, upper_ref, gain_ref, o_ref,
                    qm_ref, acc_ref, carry_ref, cmin_ref):
    j = pl.program_id(1)
    blk, gw = q_ref.shape
    half = gw // 2
    n_heads = gw // HEAD_DIM
    rows = n_heads * blk
    lane = lax.broadcasted_iota(jnp.int32, (1, gw), 1)
    lane_h = lax.broadcasted_iota(jnp.int32, (1, half), 1)

    q = q_ref[...]
    for h in range(n_heads):
        qm_ref[h * blk:(h + 1) * blk, :] = jnp.where(lane // HEAD_DIM == h, q, jnp.zeros_like(q))
    upper = upper_ref[...]
    row = lax.broadcasted_iota(jnp.int32, (rows, blk), 0) & (blk - 1)
    col = lax.broadcasted_iota(jnp.int32, (rows, blk), 1)
    causal = col < row

    def tile(t, carry, diag):
        start = pl.multiple_of((j - t) * blk, blk)
        z = _dot_nt(qm_ref[...], k_ref[pl.ds(start, blk), :])
        sp = _softplus(z)
        log_sig = z - sp
        if diag:
            sp = jnp.where(causal, sp, 0.0)
        tail = _dot(sp.astype(BF16), upper)
        if carry is not None:
            tail = tail + jnp.concatenate([carry, carry], axis=1)
        w = jnp.exp(log_sig - tail)
        if diag:
            w = jnp.where(causal, w, 0.0)
        w = w.astype(BF16)
        rowsum = jnp.broadcast_to(jnp.sum(sp, axis=-1, keepdims=True), (rows, half))
        pv = []
        for p in range(n_heads // 2):
            r = _dot(w[2 * p * blk:(2 * p + 2) * blk],
                     v_ref[pl.ds(start, blk), p * half:(p + 1) * half])
            pv.append(jnp.where(lane_h < HEAD_DIM, r[:blk], r[blk:]))
        return pv, rowsum if carry is None else carry + rowsum

    def first_tiles(with_next):
        pv, carry = tile(0, None, True)
        if with_next:
            pv_b, carry = tile(1, carry, False)
            pv = [a + b for a, b in zip(pv, pv_b)]
        carry_ref[...] = carry
        for p in range(n_heads // 2):
            acc_ref[p] = pv[p]
        cmin_ref[0] = jnp.min(carry)

    pl.when(j == 0)(functools.partial(first_tiles, False))
    pl.when(j > 0)(functools.partial(first_tiles, True))

    def live(cmin):
        return cmin <= EXP_UNDERFLOW

    def pair_body(state):
        i, _ = state
        pv_a, carry = tile(2 * i + 2, carry_ref[...], False)
        pv_b, carry = tile(2 * i + 3, carry, False)
        carry_ref[...] = carry
        for p in range(n_heads // 2):
            acc_ref[p] += pv_a[p] + pv_b[p]
        return i + 1, jnp.min(carry)

    _, cmin = lax.while_loop(lambda s: jnp.logical_and(s[0] < (j - 1) // 2, live(s[1])),
                             pair_body, (0, cmin_ref[0]))

    @pl.when(jnp.logical_and(jnp.logical_and(j >= 2, j % 2 == 0), live(cmin)))
    def _():
        pv, _ = tile(j, carry_ref[...], False)
        for p in range(n_heads // 2):
            acc_ref[p] += pv[p]

    y = jnp.concatenate([acc_ref[p] for p in range(n_heads // 2)], axis=-1)
    o_ref[...] = _rms(y, gain_ref[...]).astype(o_ref.dtype)


def _sb_attn(pd, upper, gain, batch, seq):
    m = pd.shape[0]
    gw = GROUP_WIDTH
    blk = SB_BLOCK
    nq = seq // blk
    n_heads = gw // HEAD_DIM
    whole = pl.BlockSpec(memory_space=pltpu.VMEM)
    return pl.pallas_call(
        _sb_attn_kernel,
        grid=(batch, nq),
        in_specs=[
            pl.BlockSpec((blk, gw), lambda b, j: (b * nq + j, 0)),
            pl.BlockSpec((seq, gw), lambda b, j: (b, 1)),
            pl.BlockSpec((seq, gw), lambda b, j: (b, 2)),
            whole, whole,
        ],
        out_specs=pl.BlockSpec((blk, gw), lambda b, j: (b * nq + j, 0)),
        out_shape=jax.ShapeDtypeStruct((m, gw), BF16),
        scratch_shapes=[pltpu.VMEM((n_heads * blk, gw), BF16),
                        pltpu.VMEM((n_heads // 2, blk, gw // 2), F32),
                        pltpu.VMEM((n_heads * blk, gw // 2), F32),
                        pltpu.SMEM((1,), F32)],
        compiler_params=pltpu.CompilerParams(
            dimension_semantics=("arbitrary", "arbitrary"), vmem_limit_bytes=VMEM_LIMIT_MIX),
        name="sb_attn",
    )(pd, pd, pd, upper, gain)


def _ffn_body(x_ref, yabc, yd_ref, wout_ref, g_ref, wgu_ref, wdown_ref, gfin_ref,
              o_ref, x2_ref, act_ref, final_norm, side_work):
    n_abc = yabc.shape[1]
    d_ff = wdown_ref.shape[0]
    x2_ref[...] = (x_ref[...]
                   + _dot(yabc, wout_ref[0:n_abc, :])
                   + _dot(yd_ref[...], wout_ref[n_abc:, :]))
    h = _rms(x2_ref[...], g_ref[...]).astype(BF16)
    for c in range(d_ff // FF_CHUNK):
        c0 = c * FF_CHUNK
        gate = _dot(h, wgu_ref[:, c0:c0 + FF_CHUNK])
        up = _dot(h, wgu_ref[:, d_ff + c0:d_ff + c0 + FF_CHUNK])
        act_ref[:, c0:c0 + FF_CHUNK] = (gate * (1.0 / (1.0 + jnp.exp(-gate))) * up).astype(BF16)
        if c % 2 == 0:
            next(side_work, None)
    for _ in side_work:
        pass
    out = x2_ref[...] + _dot(act_ref[...], wdown_ref[...])
    if final_norm:
        out = _rms(out, gfin_ref[...])
    o_ref[...] = out


def _mix_ffn_kernel(relb_ref, sink_ref,
                    pa_ref, halo_ref, pc_ref, kh_ref, vh_ref,
                    sguw_ref, sgub_ref, avg_ref, poolw_ref, pscale_ref, bucket_ref, gain_ref,
                    x_ref, yd_ref, wout_ref, g_ref, wgu_ref, wdown_ref, gfin_ref,
                    o_ref,
                    bias_ref, wcat_ref, yabc_ref, x2_ref, act_ref, *, blocks_per_seq, final_norm):
    i = pl.program_id(0)
    n_blocks = pl.num_programs(0) - 1

    @pl.when(i == 0)
    def _():
        yabc_ref[...] = jnp.zeros_like(yabc_ref)
        _local_mix_tables(relb_ref, sguw_ref, bucket_ref, bias_ref, wcat_ref)

    yabc_prev = yabc_ref[...]
    j = jnp.minimum(i, n_blocks - 1) % blocks_per_seq
    mixers = _local_mix_body(j, sink_ref, pa_ref, halo_ref, pc_ref, kh_ref, vh_ref,
                             sgub_ref, avg_ref, poolw_ref, pscale_ref, gain_ref,
                             yabc_ref, bias_ref, wcat_ref)
    _ffn_body(x_ref, yabc_prev, yd_ref, wout_ref, g_ref, wgu_ref, wdown_ref, gfin_ref,
              o_ref, x2_ref, act_ref, final_norm, mixers)


def _mix_ffn(x2d, pa, pc, yd, layer, rel_bias, sinks, sgu_w, sgu_b_tile, avg, pool_bd, pool_scale,
             bucket, gain, w_out, g, w_gu, w_down, g_final, seq, final_norm):
    m, d = x2d.shape
    rows = LOCAL_ROWS
    n_blocks = m // rows
    gw = GROUP_WIDTH
    n_heads = gw // HEAD_DIM
    d_ff = w_down.shape[1]
    halo_per_blk = rows // POOL_HALO
    kv_per_blk = rows // CHUNK

    def mix_blk(i):
        return jnp.minimum(i, n_blocks - 1)

    def ffn_blk(i):
        return jnp.maximum(i - 1, 0)

    smem = pl.BlockSpec(memory_space=pltpu.SMEM)
    whole = pl.BlockSpec(memory_space=pltpu.VMEM)
    vec = pl.BlockSpec((1, d), lambda i: (0, 0))

    def layer_weight(w):
        return pl.BlockSpec((None,) + w.shape[1:], lambda i: (layer, 0, 0),
                            pipeline_mode=pl.Buffered(1))

    return pl.pallas_call(
        functools.partial(_mix_ffn_kernel, blocks_per_seq=seq // rows, final_norm=final_norm),
        grid=(n_blocks + 1,),
        in_specs=[
            smem, smem,
            pl.BlockSpec((rows, 3 * gw), lambda i: (mix_blk(i), 0)),
            pl.BlockSpec((POOL_HALO, gw),
                         lambda i: (jnp.maximum(mix_blk(i) * halo_per_blk - 1, 0), 2)),
            pl.BlockSpec((rows, 3 * gw), lambda i: (mix_blk(i), 0)),
            pl.BlockSpec((CHUNK, gw), lambda i: (jnp.maximum(mix_blk(i) * kv_per_blk - 1, 0), 1)),
            pl.BlockSpec((CHUNK, gw), lambda i: (jnp.maximum(mix_blk(i) * kv_per_blk - 1, 0), 2)),
            whole, whole, whole, whole, whole, whole, whole,
            pl.BlockSpec((rows, d), lambda i: (ffn_blk(i), 0)),
            pl.BlockSpec((rows, gw), lambda i: (ffn_blk(i), 0)),
            layer_weight(w_out), vec, layer_weight(w_gu), layer_weight(w_down), vec,
        ],
        out_specs=pl.BlockSpec((rows, d), lambda i: (ffn_blk(i), 0)),
        out_shape=jax.ShapeDtypeStruct((m, d), F32),
        scratch_shapes=[pltpu.VMEM((n_heads * CHUNK, 2 * CHUNK), F32),
                        pltpu.VMEM((CHUNK, n_heads * CHUNK), BF16),
                        pltpu.VMEM((rows, 3 * gw), BF16),
                        pltpu.VMEM((rows, d), F32),
                        pltpu.VMEM((rows, d_ff), BF16)],
        compiler_params=pltpu.CompilerParams(
            dimension_semantics=("arbitrary",), vmem_limit_bytes=VMEM_LIMIT_DENSE),
        name="mix_ffn",
    )(rel_bias, sinks, pa, pa, pc, pc, pc, sgu_w, sgu_b_tile, avg, pool_bd, pool_scale, bucket, gain,
      x2d, yd, w_out, g, w_gu, w_down, g_final)


def _t5_bucket(dist):
    max_exact = N_BUCKETS // 2
    df = jnp.maximum(dist, 1).astype(F32)
    large = max_exact + (jnp.log(df / max_exact) / math.log(MAX_DISTANCE / max_exact)
                         * (N_BUCKETS - max_exact)).astype(jnp.int32)
    large = jnp.minimum(large, N_BUCKETS - 1)
    return jnp.where(dist < max_exact, dist, large)


def kernel(x, w_in, w_out, sgu_w, sgu_b, pool_w, pool_scale, swa_sinks, rel_bias, mix_out_gain,
           norm_mix, norm_ffn, w_gate_up, w_down, norm_final):
    batch, seq, d = x.shape
    depth = w_in.shape[0]
    gw = GROUP_WIDTH
    n_pool = len(POOL_WINDOWS)
    assert seq % LOCAL_ROWS == 0 and seq % SB_BLOCK == 0 and (batch * seq) % DENSE_ROWS == 0
    assert LOCAL_ROWS == DENSE_ROWS

    head_id = np.arange(gw) // HEAD_DIM
    avg = (head_id[:, None] == head_id[None, :]) / HEAD_DIM
    avg2 = jnp.asarray(np.concatenate([avg, avg], axis=0), BF16)
    upper = jnp.asarray(np.arange(SB_BLOCK)[:, None] > np.arange(SB_BLOCK)[None, :], BF16)
    dist = (jnp.arange(CHUNK)[:, None] + CHUNK) - jnp.arange(2 * CHUNK)[None, :]
    bucket = _t5_bucket(jnp.clip(dist, 0, CHUNK - 1)).astype(jnp.int32)

    w_in_b, w_out_b = w_in.astype(BF16), w_out.astype(BF16)
    w_gu_b, w_down_b = w_gate_up.astype(BF16), w_down.astype(BF16)
    pool_bd = (pool_w[:, :, :, None, :] * jnp.eye(n_pool, dtype=F32)[None, :, None, :, None])
    pool_bd = pool_bd.reshape(depth, gw, gw).astype(BF16)
    sgu_b_tile = jnp.repeat(jnp.swapaxes(sgu_b, 1, 2), HEAD_DIM, axis=2)
    gains = mix_out_gain.reshape(depth, 1, 4 * gw)

    x2d = x.reshape(batch * seq, d)
    for l in range(depth):
        pa, pc, pd = _norm_proj(x2d, norm_mix[l].reshape(1, d), w_in_b[l])
        yd = _sb_attn(pd, upper, gains[l, :, 3 * gw:], batch, seq)
        x2d = _mix_ffn(x2d, pa, pc, yd, l, rel_bias, swa_sinks[l], sgu_w[l], sgu_b_tile[l], avg2,
                       pool_bd[l], pool_scale[l].reshape(1, gw), bucket, gains[l, :, :3 * gw],
                       w_out_b, norm_ffn[l].reshape(1, d), w_gu_b, w_down_b,
                       norm_final.reshape(1, d), seq, l == depth - 1)
    return x2d.reshape(batch, seq, d)
```

```python
import functools
import math

import numpy as np
import jax
import jax.numpy as jnp
from jax import lax
from jax.experimental import pallas as pl
from jax.experimental.pallas import tpu as pltpu

F32 = jnp.float32
BF16 = jnp.bfloat16

HEAD_DIM = 64
GROUP_WIDTH = 256
CHUNK = 128
POOL_WINDOWS = (2, 4, 8, 16)
POOL_HALO = 16
N_BUCKETS = 32
MAX_DISTANCE = 128
EPS = 1e-6
MASK_VALUE = -1e30
EXP_UNDERFLOW = 104.0

LOCAL_ROWS = 512
SB_BLOCK = 256
PROJ_ROWS = 1024
FFN_ROWS = 512
FF_CHUNK = 256
VMEM_LIMIT_DENSE = 48 * 1024 * 1024
VMEM_LIMIT_MIX = 40 * 1024 * 1024


def _rms(x, g):
    return x * lax.rsqrt(jnp.mean(x * x, axis=-1, keepdims=True) + EPS) * g


def _gelu_tanh(x):
    inner = math.sqrt(2.0 / math.pi) * (x + 0.044715 * (x * x * x))
    return x * (0.5 * (1.0 + jnp.tanh(inner)))


def _split_bf16(x):
    hi = x.astype(BF16)
    lo = (x - hi.astype(F32)).astype(BF16)
    return hi, lo


def _dot(a, b):
    return jnp.dot(a, b, preferred_element_type=F32)


def _dot_nt(a, b):
    return lax.dot_general(a, b, (((1,), (1,)), ((), ())), preferred_element_type=F32)


def _layer_weight(w, layer):
    return pl.BlockSpec((None,) + w.shape[1:], lambda *_: (layer, 0, 0),
                        pipeline_mode=pl.Buffered(1))


def _per_query_head(kv):
    lane = lax.broadcasted_iota(jnp.int32, (1, kv.shape[1]), 1)
    swapped = pltpu.roll(kv, HEAD_DIM, 1)
    first = jnp.where(lane < HEAD_DIM, kv, swapped)
    second = jnp.where(lane < HEAD_DIM, swapped, kv)
    return jnp.concatenate([first, second], axis=1)


def _norm_proj_kernel(x_ref, g_ref, w_ref, pa_ref, pc_ref, pd_ref):
    gw = GROUP_WIDTH
    kvw = w_ref.shape[1] - 7 * gw
    scale = HEAD_DIM ** -0.5
    h = _rms(x_ref[...], g_ref[...]).astype(BF16)
    y = _dot(h, w_ref[...])
    pa_ref[...] = y[:, :3 * gw]
    c_k = 4 * gw
    c_v = c_k + kvw // 2
    d_q = c_v + kvw // 2
    pc_ref[:, 0:gw] = (y[:, 3 * gw:4 * gw] * scale).astype(BF16)
    pc_ref[:, gw:2 * gw] = _per_query_head(y[:, c_k:c_v]).astype(BF16)
    pc_ref[:, 2 * gw:3 * gw] = _per_query_head(y[:, c_v:d_q]).astype(BF16)
    pd_ref[:, 0:gw] = (y[:, d_q:d_q + gw] * scale).astype(BF16)
    pd_ref[:, gw:3 * gw] = y[:, d_q + gw:].astype(BF16)


def _norm_proj(x2d, g, w, layer):
    m, d = x2d.shape
    tm = PROJ_ROWS
    na = nc = nd = 3 * GROUP_WIDTH
    assert w.shape[2] == 7 * GROUP_WIDTH + 4 * HEAD_DIM, "expects 2 kv heads of HEAD_DIM"
    return pl.pallas_call(
        _norm_proj_kernel,
        grid=(m // tm,),
        in_specs=[
            pl.BlockSpec((tm, d), lambda i: (i, 0)),
            pl.BlockSpec((1, d), lambda i: (0, 0)),
            _layer_weight(w, layer),
        ],
        out_specs=[
            pl.BlockSpec((tm, na), lambda i: (i, 0)),
            pl.BlockSpec((tm, nc), lambda i: (i, 0)),
            pl.BlockSpec((tm, nd), lambda i: (i, 0)),
        ],
        out_shape=[
            jax.ShapeDtypeStruct((m, na), F32),
            jax.ShapeDtypeStruct((m, nc), BF16),
            jax.ShapeDtypeStruct((m, nd), BF16),
        ],
        compiler_params=pltpu.CompilerParams(
            dimension_semantics=("arbitrary",), vmem_limit_bytes=VMEM_LIMIT_MIX),
        name="norm_proj",
    )(x2d, g, w)


def _mask_heads(x, n_heads):
    lane = lax.broadcasted_iota(jnp.int32, (1, x.shape[1]), 1)
    return jnp.concatenate(
        [jnp.where(lane // HEAD_DIM == h, x, jnp.zeros_like(x)) for h in range(n_heads)], axis=0)


def _local_mix_kernel(relb_ref, sink_ref,
                      pa_ref, halo_ref, pc_ref, kh_ref, vh_ref,
                      sguw_ref, sgub_ref, avg_ref, poolw_ref, pscale_ref,
                      bucket_ref, gain_ref,
                      o_ref,
                      bias_ref, wcat_ref):
    b = pl.program_id(0)
    j = pl.program_id(1)
    rows = pa_ref.shape[0]
    gw = GROUP_WIDTH
    n_heads = gw // HEAD_DIM
    n_chunks = rows // CHUNK
    lane = lax.broadcasted_iota(jnp.int32, (1, gw), 1)

    @pl.when(jnp.logical_and(b == 0, j == 0))
    def _():
        bucket = bucket_ref[...]
        tril = (lax.broadcasted_iota(jnp.int32, (CHUNK, CHUNK), 1)
                <= lax.broadcasted_iota(jnp.int32, (CHUNK, CHUNK), 0))
        for h in range(n_heads):
            acc = jnp.zeros(bucket.shape, F32)
            for bk in range(N_BUCKETS):
                acc = jnp.where(bucket == bk, relb_ref[bk, h], acc)
            bias_ref[h * CHUNK:(h + 1) * CHUNK, :] = acc
            wcat_ref[:, h * CHUNK:(h + 1) * CHUNK] = jnp.where(tril, sguw_ref[h], 0.0).astype(BF16)

    gain = gain_ref[...]

    p = pa_ref[:, 2 * gw:3 * gw]
    halo = jnp.where(j == 0, 0.0, halo_ref[...])
    ext = jnp.concatenate([halo, p], axis=0)
    sums = []
    run = ext
    for shift in (1, 2, 4, 8):
        run = run + pltpu.roll(run, shift, 0)
        sums.append(run[POOL_HALO:])
    group = lane // (gw // len(POOL_WINDOWS))
    wsum = jnp.where(group == 0, sums[0],
                     jnp.where(group == 1, sums[1],
                               jnp.where(group == 2, sums[2], sums[3])))
    win = jnp.where(group == 0, POOL_WINDOWS[0],
                    jnp.where(group == 1, POOL_WINDOWS[1],
                              jnp.where(group == 2, POOL_WINDOWS[2], POOL_WINDOWS[3])))
    t_seq = j * rows + lax.broadcasted_iota(jnp.int32, (rows, 1), 0)
    count = jnp.minimum(t_seq + 1, win).astype(F32)
    yb = wsum / count - p
    yb = _dot(yb.astype(BF16), poolw_ref[...]) * pscale_ref[...]
    o_ref[:, gw:2 * gw] = _rms(yb, gain[:, gw:2 * gw]).astype(o_ref.dtype)

    avg = avg_ref[...]
    u = _gelu_tanh(pa_ref[:, 0:gw])
    v = _gelu_tanh(pa_ref[:, gw:2 * gw])
    v_hi, v_lo = _split_bf16(v)
    vc = v - _dot(jnp.concatenate([v_hi, v_lo], axis=1), avg)
    sq_hi, sq_lo = _split_bf16(vc * vc)
    var = _dot(jnp.concatenate([sq_hi, sq_lo], axis=1), avg)
    vn = (vc * lax.rsqrt(var + EPS)).astype(BF16)
    rhs = jnp.concatenate(
        [_mask_heads(vn[c * CHUNK:(c + 1) * CHUNK], n_heads) for c in range(n_chunks)], axis=1)
    mix = _dot(wcat_ref[...], rhs)
    for c in range(n_chunks):
        r0 = c * CHUNK
        ya = u[r0:r0 + CHUNK] * (mix[:, c * gw:(c + 1) * gw] + sgub_ref[...])
        o_ref[r0:r0 + CHUNK, 0:gw] = _rms(ya, gain[:, 0:gw]).astype(o_ref.dtype)

    stacked = (n_heads * CHUNK, 2 * CHUNK)
    qi = lax.broadcasted_iota(jnp.int32, stacked, 0) & (CHUNK - 1)
    ki = lax.broadcasted_iota(jnp.int32, stacked, 1)
    in_window = jnp.logical_and(ki > qi, ki <= qi + CHUNK)
    head_of_row = lax.broadcasted_iota(jnp.int32, (n_heads * CHUNK, 1), 0) // CHUNK
    sink = jnp.zeros((n_heads * CHUNK, 1), F32)
    for h in range(n_heads):
        sink = jnp.where(head_of_row == h, sink_ref[h], sink)
    bias = bias_ref[...]
    for c in range(n_chunks):
        r0 = c * CHUNK
        if c == 0:
            k_prev, v_prev = kh_ref[...], vh_ref[...]
            mask = jnp.logical_and(in_window, jnp.logical_or(ki >= CHUNK, j > 0))
        else:
            k_prev = pc_ref[r0 - CHUNK:r0, gw:2 * gw]
            v_prev = pc_ref[r0 - CHUNK:r0, 2 * gw:3 * gw]
            mask = in_window
        k2 = jnp.concatenate([k_prev, pc_ref[r0:r0 + CHUNK, gw:2 * gw]], axis=0)
        v2 = jnp.concatenate([v_prev, pc_ref[r0:r0 + CHUNK, 2 * gw:3 * gw]], axis=0)
        qm = _mask_heads(pc_ref[r0:r0 + CHUNK, 0:gw], n_heads)
        logits = jnp.where(mask, _dot_nt(qm, k2) + bias, MASK_VALUE)
        mx = jnp.maximum(jnp.max(logits, axis=-1, keepdims=True), sink)
        e = jnp.exp(logits - mx)
        denom = jnp.sum(e, axis=-1, keepdims=True) + jnp.exp(sink - mx)
        probs = (e * (1.0 / denom)).astype(BF16)
        pcat = jnp.concatenate([probs[h * CHUNK:(h + 1) * CHUNK] for h in range(n_heads)], axis=1)
        yc = _dot(pcat, _mask_heads(v2, n_heads))
        o_ref[r0:r0 + CHUNK, 2 * gw:3 * gw] = _rms(yc, gain[:, 2 * gw:3 * gw]).astype(o_ref.dtype)


def _local_mix(pa, pc, rel_bias, sinks, sgu_w, sgu_b_tile, avg, pool_bd, pool_scale,
               bucket, gain, batch, seq):
    m = pa.shape[0]
    rows = LOCAL_ROWS
    nq = seq // rows
    gw = GROUP_WIDTH
    n_heads = gw // HEAD_DIM

    def row_blk(b, j):
        return b * nq + j

    halo_per_blk = rows // POOL_HALO
    kv_per_blk = rows // CHUNK
    smem = pl.BlockSpec(memory_space=pltpu.SMEM)
    whole = pl.BlockSpec(memory_space=pltpu.VMEM)
    return pl.pallas_call(
        _local_mix_kernel,
        grid=(batch, nq),
        in_specs=[
            smem, smem,
            pl.BlockSpec((rows, 3 * gw), lambda b, j: (row_blk(b, j), 0)),
            pl.BlockSpec((POOL_HALO, gw),
                         lambda b, j: (jnp.maximum(row_blk(b, j) * halo_per_blk - 1, 0), 2)),
            pl.BlockSpec((rows, 3 * gw), lambda b, j: (row_blk(b, j), 0)),
            pl.BlockSpec((CHUNK, gw),
                         lambda b, j: (jnp.maximum(row_blk(b, j) * kv_per_blk - 1, 0), 1)),
            pl.BlockSpec((CHUNK, gw),
                         lambda b, j: (jnp.maximum(row_blk(b, j) * kv_per_blk - 1, 0), 2)),
            whole, whole, whole, whole, whole, whole, whole,
        ],
        out_specs=pl.BlockSpec((rows, 3 * gw), lambda b, j: (row_blk(b, j), 0)),
        out_shape=jax.ShapeDtypeStruct((m, 3 * gw), BF16),
        scratch_shapes=[pltpu.VMEM((n_heads * CHUNK, 2 * CHUNK), F32),
                        pltpu.VMEM((CHUNK, n_heads * CHUNK), BF16)],
        compiler_params=pltpu.CompilerParams(
            dimension_semantics=("arbitrary", "arbitrary"), vmem_limit_bytes=VMEM_LIMIT_MIX),
        name="local_mix",
    )(rel_bias, sinks, pa, pa, pc, pc, pc, sgu_w, sgu_b_tile, avg, pool_bd, pool_scale,
      bucket, gain)


def _softplus(z):
    neg_abs = lax.bitcast_convert_type(
        lax.bitcast_convert_type(z, jnp.uint32) | jnp.uint32(0x80000000), F32)
    return jnp.maximum(z, 0.0) + jnp.log(1.0 + jnp.exp(neg_abs))


def _sb_attn_kernel(q_ref, k_ref, v_ref, upper_ref, gain_ref, o_ref,
                    qm_ref, acc_ref, carry_ref, cmin_ref):
    j = pl.program_id(1)
    blk, gw = q_ref.shape
    half = gw // 2
    n_heads = gw // HEAD_DIM
    rows = n_heads * blk
    lane = lax.broadcasted_iota(jnp.int32, (1, gw), 1)
    lane_h = lax.broadcasted_iota(jnp.int32, (1, half), 1)

    q = q_ref[...]
    for h in range(n_heads):
        qm_ref[h * blk:(h + 1) * blk, :] = jnp.where(lane // HEAD_DIM == h, q, jnp.zeros_like(q))
    upper = upper_ref[...]
    causal = (lax.broadcasted_iota(jnp.int32, (blk, blk), 1)
              < lax.broadcasted_iota(jnp.int32, (blk, blk), 0))

    def keep_causal(x):
        return jnp.where(causal, x.reshape(n_heads, blk, blk), 0.0).reshape(rows, blk)

    def tile(t, carry, diag):
        start = pl.multiple_of((j - t) * blk, blk)
        z = _dot_nt(qm_ref[...], k_ref[pl.ds(start, blk), :])
        sp = _softplus(z)
        log_sig = z - sp
        if diag:
            sp = keep_causal(sp)
        tail = _dot(sp.astype(BF16), upper)
        if carry is not None:
            tail = tail + jnp.concatenate([carry, carry], axis=1)
        w = jnp.exp(log_sig - tail)
        if diag:
            w = keep_causal(w)
        w = w.astype(BF16)
        rowsum = jnp.broadcast_to(jnp.sum(sp, axis=-1, keepdims=True), (rows, half))
        pv = []
        for p in range(n_heads // 2):
            r = _dot(w[2 * p * blk:(2 * p + 2) * blk],
                     v_ref[pl.ds(start, blk), p * half:(p + 1) * half])
            pv.append(jnp.where(lane_h < HEAD_DIM, r[:blk], r[blk:]))
        return pv, rowsum if carry is None else carry + rowsum

    def first_tiles(with_next):
        pv, carry = tile(0, None, True)
        if with_next:
            pv_b, carry = tile(1, carry, False)
            pv = [a + b for a, b in zip(pv, pv_b)]
        carry_ref[...] = carry
        for p in range(n_heads // 2):
            acc_ref[p] = pv[p]
        cmin_ref[0] = jnp.min(carry)

    pl.when(j == 0)(functools.partial(first_tiles, False))
    pl.when(j > 0)(functools.partial(first_tiles, True))

    def live(cmin):
        return cmin <= EXP_UNDERFLOW

    def pair_body(state):
        i, _ = state
        pv_a, carry = tile(2 * i + 2, carry_ref[...], False)
        pv_b, carry = tile(2 * i + 3, carry, False)
        carry_ref[...] = carry
        for p in range(n_heads // 2):
            acc_ref[p] += pv_a[p] + pv_b[p]
        return i + 1, jnp.min(carry)

    _, cmin = lax.while_loop(lambda s: jnp.logical_and(s[0] < (j - 1) // 2, live(s[1])),
                             pair_body, (0, cmin_ref[0]))

    @pl.when(jnp.logical_and(jnp.logical_and(j >= 2, j % 2 == 0), live(cmin)))
    def _():
        pv, _ = tile(j, carry_ref[...], False)
        for p in range(n_heads // 2):
            acc_ref[p] += pv[p]

    y = jnp.concatenate([acc_ref[p] for p in range(n_heads // 2)], axis=-1)
    o_ref[...] = _rms(y, gain_ref[...]).astype(o_ref.dtype)


def _sb_attn(pd, upper, gain, batch, seq):
    m = pd.shape[0]
    gw = GROUP_WIDTH
    blk = SB_BLOCK
    nq = seq // blk
    n_heads = gw // HEAD_DIM
    whole = pl.BlockSpec(memory_space=pltpu.VMEM)
    return pl.pallas_call(
        _sb_attn_kernel,
        grid=(batch, nq),
        in_specs=[
            pl.BlockSpec((blk, gw), lambda b, j: (b * nq + j, 0)),
            pl.BlockSpec((seq, gw), lambda b, j: (b, 1)),
            pl.BlockSpec((seq, gw), lambda b, j: (b, 2)),
            whole, whole,
        ],
        out_specs=pl.BlockSpec((blk, gw), lambda b, j: (b * nq + j, 0)),
        out_shape=jax.ShapeDtypeStruct((m, gw), BF16),
        scratch_shapes=[pltpu.VMEM((n_heads * blk, gw), BF16),
                        pltpu.VMEM((n_heads // 2, blk, gw // 2), F32),
                        pltpu.VMEM((n_heads * blk, gw // 2), F32),
                        pltpu.SMEM((1,), F32)],
        compiler_params=pltpu.CompilerParams(
            dimension_semantics=("arbitrary", "arbitrary"), vmem_limit_bytes=VMEM_LIMIT_MIX),
        name="sb_attn",
    )(pd, pd, pd, upper, gain)


def _out_ffn_kernel(x_ref, yabc_ref, yd_ref, wout_ref, g_ref, wgu_ref, wdown_ref, gfin_ref,
                    o_ref, x2_ref, act_ref, *, final_norm):
    n_abc = yabc_ref.shape[1]
    d_ff = wdown_ref.shape[0]
    x2_ref[...] = (x_ref[...]
                   + _dot(yabc_ref[...], wout_ref[0:n_abc, :])
                   + _dot(yd_ref[...], wout_ref[n_abc:, :]))
    h = _rms(x2_ref[...], g_ref[...]).astype(BF16)
    for c in range(d_ff // FF_CHUNK):
        c0 = c * FF_CHUNK
        gate = _dot(h, wgu_ref[:, c0:c0 + FF_CHUNK])
        up = _dot(h, wgu_ref[:, d_ff + c0:d_ff + c0 + FF_CHUNK])
        act_ref[:, c0:c0 + FF_CHUNK] = (gate * (1.0 / (1.0 + jnp.exp(-gate))) * up).astype(BF16)
    out = x2_ref[...] + _dot(act_ref[...], wdown_ref[...])
    if final_norm:
        out = _rms(out, gfin_ref[...])
    o_ref[...] = out


def _out_ffn(x2d, yabc, yd, w_out, g, w_gu, w_down, g_final, layer, final_norm):
    m, d = x2d.shape
    tm = FFN_ROWS
    vec = pl.BlockSpec((1, d), lambda i: (0, 0))
    return pl.pallas_call(
        functools.partial(_out_ffn_kernel, final_norm=final_norm),
        grid=(m // tm,),
        in_specs=[
            pl.BlockSpec((tm, d), lambda i: (i, 0)),
            pl.BlockSpec((tm, yabc.shape[1]), lambda i: (i, 0)),
            pl.BlockSpec((tm, yd.shape[1]), lambda i: (i, 0)),
            _layer_weight(w_out, layer), vec,
            _layer_weight(w_gu, layer), _layer_weight(w_down, layer), vec,
        ],
        out_specs=pl.BlockSpec((tm, d), lambda i: (i, 0)),
        out_shape=jax.ShapeDtypeStruct((m, d), F32),
        scratch_shapes=[pltpu.VMEM((tm, d), F32), pltpu.VMEM((tm, w_down.shape[1]), BF16)],
        compiler_params=pltpu.CompilerParams(
            dimension_semantics=("arbitrary",), vmem_limit_bytes=VMEM_LIMIT_DENSE),
        name="out_ffn",
    )(x2d, yabc, yd, w_out, g, w_gu, w_down, g_final)


def _t5_bucket(dist):
    max_exact = N_BUCKETS // 2
    df = jnp.maximum(dist, 1).astype(F32)
    large = max_exact + (jnp.log(df / max_exact) / math.log(MAX_DISTANCE / max_exact)
                         * (N_BUCKETS - max_exact)).astype(jnp.int32)
    large = jnp.minimum(large, N_BUCKETS - 1)
    return jnp.where(dist < max_exact, dist, large)


def kernel(x, w_in, w_out, sgu_w, sgu_b, pool_w, pool_scale, swa_sinks, rel_bias, mix_out_gain,
           norm_mix, norm_ffn, w_gate_up, w_down, norm_final):
    batch, seq, d = x.shape
    depth = w_in.shape[0]
    gw = GROUP_WIDTH
    n_pool = len(POOL_WINDOWS)
    rows = batch * seq
    assert seq % LOCAL_ROWS == 0 and seq % SB_BLOCK == 0
    assert rows % PROJ_ROWS == 0 and rows % FFN_ROWS == 0

    head_id = np.arange(gw) // HEAD_DIM
    avg = (head_id[:, None] == head_id[None, :]) / HEAD_DIM
    avg2 = jnp.asarray(np.concatenate([avg, avg], axis=0), BF16)
    upper = jnp.asarray(np.arange(SB_BLOCK)[:, None] > np.arange(SB_BLOCK)[None, :], BF16)
    dist = (jnp.arange(CHUNK)[:, None] + CHUNK) - jnp.arange(2 * CHUNK)[None, :]
    bucket = _t5_bucket(jnp.clip(dist, 0, CHUNK - 1)).astype(jnp.int32)

    w_in_b, w_out_b = w_in.astype(BF16), w_out.astype(BF16)
    w_gu_b, w_down_b = w_gate_up.astype(BF16), w_down.astype(BF16)
    pool_bd = (pool_w[:, :, :, None, :] * jnp.eye(n_pool, dtype=F32)[None, :, None, :, None])
    pool_bd = pool_bd.reshape(depth, gw, gw).astype(BF16)
    sgu_b_tile = jnp.repeat(jnp.swapaxes(sgu_b, 1, 2), HEAD_DIM, axis=2)
    gains = mix_out_gain.reshape(depth, 1, 4 * gw)

    x2d = x.reshape(rows, d)
    for l in range(depth):
        pa, pc, pd = _norm_proj(x2d, norm_mix[l].reshape(1, d), w_in_b, l)
        yabc = _local_mix(pa, pc, rel_bias, swa_sinks[l], sgu_w[l], sgu_b_tile[l], avg2,
                          pool_bd[l], pool_scale[l].reshape(1, gw), bucket,
                          gains[l, :, :3 * gw], batch, seq)
        yd = _sb_attn(pd, upper, gains[l, :, 3 * gw:], batch, seq)
        x2d = _out_ffn(x2d, yabc, yd, w_out_b, norm_ffn[l].reshape(1, d), w_gu_b, w_down_b,
                       norm_final.reshape(1, d), l, l == depth - 1)
    return x2d.reshape(batch, seq, d)
```

```python
import functools
import math

import numpy as np
import jax
import jax.numpy as jnp
from jax import lax
from jax.experimental import pallas as pl
from jax.experimental.pallas import tpu as pltpu

F32 = jnp.float32
BF16 = jnp.bfloat16

HEAD_DIM = 64
GROUP_WIDTH = 256
CHUNK = 128
POOL_WINDOWS = (2, 4, 8, 16)
POOL_HALO = 16
N_BUCKETS = 32
MAX_DISTANCE = 128
EPS = 1e-6
MASK_VALUE = -1e30
EXP_UNDERFLOW = 104.0

LOCAL_ROWS = 512
SB_BLOCK = 256
SB_SUBBLOCKS = 2
PROJ_ROWS = 512
FFN_ROWS = 512
FF_CHUNK = 256
VMEM_LIMIT_DENSE = 48 * 1024 * 1024
VMEM_LIMIT_MIX = 40 * 1024 * 1024


def _rms(x, g):
    return x * lax.rsqrt(jnp.mean(x * x, axis=-1, keepdims=True) + EPS) * g


def _gelu_tanh(x):
    inner = math.sqrt(2.0 / math.pi) * (x + 0.044715 * (x * x * x))
    return x * (0.5 * (1.0 + jnp.tanh(inner)))


def _split_bf16(x):
    hi = x.astype(BF16)
    lo = (x - hi.astype(F32)).astype(BF16)
    return hi, lo


def _dot(a, b):
    return jnp.dot(a, b, preferred_element_type=F32)


def _dot_nt(a, b):
    return lax.dot_general(a, b, (((1,), (1,)), ((), ())), preferred_element_type=F32)


def _layer_weight(w, layer):
    return pl.BlockSpec((None,) + w.shape[1:], lambda *_: (layer, 0, 0),
                        pipeline_mode=pl.Buffered(1))


def _per_query_head(kv):
    lane = lax.broadcasted_iota(jnp.int32, (1, kv.shape[1]), 1)
    swapped = pltpu.roll(kv, HEAD_DIM, 1)
    first = jnp.where(lane < HEAD_DIM, kv, swapped)
    second = jnp.where(lane < HEAD_DIM, swapped, kv)
    return jnp.concatenate([first, second], axis=1)


def _norm_proj_kernel(x_ref, g_ref, w_ref, pa_ref, pc_ref, pd_ref):
    gw = GROUP_WIDTH
    kvw = w_ref.shape[1] - 7 * gw
    scale = HEAD_DIM ** -0.5
    h = _rms(x_ref[...], g_ref[...]).astype(BF16)
    y = _dot(h, w_ref[...])
    pa_ref[...] = y[:, :3 * gw]
    c_k = 4 * gw
    c_v = c_k + kvw // 2
    d_q = c_v + kvw // 2
    pc_ref[:, 0:gw] = (y[:, 3 * gw:4 * gw] * scale).astype(BF16)
    pc_ref[:, gw:2 * gw] = _per_query_head(y[:, c_k:c_v]).astype(BF16)
    pc_ref[:, 2 * gw:3 * gw] = _per_query_head(y[:, c_v:d_q]).astype(BF16)
    pd_ref[:, 0:gw] = (y[:, d_q:d_q + gw] * scale).astype(BF16)
    pd_ref[:, gw:3 * gw] = y[:, d_q + gw:].astype(BF16)


def _norm_proj(x2d, g, w, layer):
    m, d = x2d.shape
    tm = PROJ_ROWS
    na = nc = nd = 3 * GROUP_WIDTH
    assert w.shape[2] == 7 * GROUP_WIDTH + 4 * HEAD_DIM, "expects 2 kv heads of HEAD_DIM"
    return pl.pallas_call(
        _norm_proj_kernel,
        grid=(m // tm,),
        in_specs=[
            pl.BlockSpec((tm, d), lambda i: (i, 0)),
            pl.BlockSpec((1, d), lambda i: (0, 0)),
            _layer_weight(w, layer),
        ],
        out_specs=[
            pl.BlockSpec((tm, na), lambda i: (i, 0)),
            pl.BlockSpec((tm, nc), lambda i: (i, 0)),
            pl.BlockSpec((tm, nd), lambda i: (i, 0)),
        ],
        out_shape=[
            jax.ShapeDtypeStruct((m, na), F32),
            jax.ShapeDtypeStruct((m, nc), BF16),
            jax.ShapeDtypeStruct((m, nd), BF16),
        ],
        compiler_params=pltpu.CompilerParams(
            dimension_semantics=("arbitrary",), vmem_limit_bytes=VMEM_LIMIT_MIX),
        name="norm_proj",
    )(x2d, g, w)


def _mask_heads(x, n_heads):
    lane = lax.broadcasted_iota(jnp.int32, (1, x.shape[1]), 1)
    return jnp.concatenate(
        [jnp.where(lane // HEAD_DIM == h, x, jnp.zeros_like(x)) for h in range(n_heads)], axis=0)


def _local_mix_kernel(relb_ref, sink_ref,
                      pa_ref, halo_ref, pc_ref, kh_ref, vh_ref,
                      sguw_ref, sgub_ref, avg_ref, poolw_ref, pscale_ref,
                      bucket_ref, gain_ref,
                      o_ref,
                      bias_ref, wcat_ref):
    b = pl.program_id(0)
    j = pl.program_id(1)
    rows = pa_ref.shape[0]
    gw = GROUP_WIDTH
    n_heads = gw // HEAD_DIM
    n_chunks = rows // CHUNK
    lane = lax.broadcasted_iota(jnp.int32, (1, gw), 1)

    @pl.when(jnp.logical_and(b == 0, j == 0))
    def _():
        bucket = bucket_ref[...]
        tril = (lax.broadcasted_iota(jnp.int32, (CHUNK, CHUNK), 1)
                <= lax.broadcasted_iota(jnp.int32, (CHUNK, CHUNK), 0))
        for h in range(n_heads):
            acc = jnp.zeros(bucket.shape, F32)
            for bk in range(N_BUCKETS):
                acc = jnp.where(bucket == bk, relb_ref[bk, h], acc)
            bias_ref[h * CHUNK:(h + 1) * CHUNK, :] = acc
            wcat_ref[:, h * CHUNK:(h + 1) * CHUNK] = jnp.where(tril, sguw_ref[h], 0.0).astype(BF16)

    gain = gain_ref[...]

    p = pa_ref[:, 2 * gw:3 * gw]
    halo = jnp.where(j == 0, 0.0, halo_ref[...])
    ext = jnp.concatenate([halo, p], axis=0)
    sums = []
    run = ext
    for shift in (1, 2, 4, 8):
        run = run + pltpu.roll(run, shift, 0)
        sums.append(run[POOL_HALO:])
    group = lane // (gw // len(POOL_WINDOWS))
    wsum = jnp.where(group == 0, sums[0],
                     jnp.where(group == 1, sums[1],
                               jnp.where(group == 2, sums[2], sums[3])))
    win = jnp.where(group == 0, POOL_WINDOWS[0],
                    jnp.where(group == 1, POOL_WINDOWS[1],
                              jnp.where(group == 2, POOL_WINDOWS[2], POOL_WINDOWS[3])))
    t_seq = j * rows + lax.broadcasted_iota(jnp.int32, (rows, 1), 0)
    count = jnp.minimum(t_seq + 1, win).astype(F32)
    yb = wsum / count - p
    yb = _dot(yb.astype(BF16), poolw_ref[...]) * pscale_ref[...]
    o_ref[:, gw:2 * gw] = _rms(yb, gain[:, gw:2 * gw]).astype(o_ref.dtype)

    avg = avg_ref[...]
    u = _gelu_tanh(pa_ref[:, 0:gw])
    v = _gelu_tanh(pa_ref[:, gw:2 * gw])
    v_hi, v_lo = _split_bf16(v)
    vc = v - _dot(jnp.concatenate([v_hi, v_lo], axis=1), avg)
    sq_hi, sq_lo = _split_bf16(vc * vc)
    var = _dot(jnp.concatenate([sq_hi, sq_lo], axis=1), avg)
    vn = (vc * lax.rsqrt(var + EPS)).astype(BF16)
    rhs = jnp.concatenate(
        [_mask_heads(vn[c * CHUNK:(c + 1) * CHUNK], n_heads) for c in range(n_chunks)], axis=1)
    mix = _dot(wcat_ref[...], rhs)
    for c in range(n_chunks):
        r0 = c * CHUNK
        ya = u[r0:r0 + CHUNK] * (mix[:, c * gw:(c + 1) * gw] + sgub_ref[...])
        o_ref[r0:r0 + CHUNK, 0:gw] = _rms(ya, gain[:, 0:gw]).astype(o_ref.dtype)

    stacked = (n_heads * CHUNK, 2 * CHUNK)
    qi = lax.broadcasted_iota(jnp.int32, stacked, 0) & (CHUNK - 1)
    ki = lax.broadcasted_iota(jnp.int32, stacked, 1)
    in_window = jnp.logical_and(ki > qi, ki <= qi + CHUNK)
    head_of_row = lax.broadcasted_iota(jnp.int32, (n_heads * CHUNK, 1), 0) // CHUNK
    sink = jnp.zeros((n_heads * CHUNK, 1), F32)
    for h in range(n_heads):
        sink = jnp.where(head_of_row == h, sink_ref[h], sink)
    bias = bias_ref[...]
    for c in range(n_chunks):
        r0 = c * CHUNK
        if c == 0:
            k_prev, v_prev = kh_ref[...], vh_ref[...]
            mask = jnp.logical_and(in_window, jnp.logical_or(ki >= CHUNK, j > 0))
        else:
            k_prev = pc_ref[r0 - CHUNK:r0, gw:2 * gw]
            v_prev = pc_ref[r0 - CHUNK:r0, 2 * gw:3 * gw]
            mask = in_window
        k2 = jnp.concatenate([k_prev, pc_ref[r0:r0 + CHUNK, gw:2 * gw]], axis=0)
        v2 = jnp.concatenate([v_prev, pc_ref[r0:r0 + CHUNK, 2 * gw:3 * gw]], axis=0)
        qm = _mask_heads(pc_ref[r0:r0 + CHUNK, 0:gw], n_heads)
        logits = jnp.where(mask, _dot_nt(qm, k2) + bias, MASK_VALUE)
        mx = jnp.maximum(jnp.max(logits, axis=-1, keepdims=True), sink)
        e = jnp.exp(logits - mx)
        denom = jnp.sum(e, axis=-1, keepdims=True) + jnp.exp(sink - mx)
        probs = (e * (1.0 / denom)).astype(BF16)
        pcat = jnp.concatenate([probs[h * CHUNK:(h + 1) * CHUNK] for h in range(n_heads)], axis=1)
        yc = _dot(pcat, _mask_heads(v2, n_heads))
        o_ref[r0:r0 + CHUNK, 2 * gw:3 * gw] = _rms(yc, gain[:, 2 * gw:3 * gw]).astype(o_ref.dtype)


def _local_mix(pa, pc, rel_bias, sinks, sgu_w, sgu_b_tile, avg, pool_bd, pool_scale,
               bucket, gain, batch, seq):
    m = pa.shape[0]
    rows = LOCAL_ROWS
    nq = seq // rows
    gw = GROUP_WIDTH
    n_heads = gw // HEAD_DIM

    def row_blk(b, j):
        return b * nq + j

    halo_per_blk = rows // POOL_HALO
    kv_per_blk = rows // CHUNK
    smem = pl.BlockSpec(memory_space=pltpu.SMEM)
    whole = pl.BlockSpec(memory_space=pltpu.VMEM)
    return pl.pallas_call(
        _local_mix_kernel,
        grid=(batch, nq),
        in_specs=[
            smem, smem,
            pl.BlockSpec((rows, 3 * gw), lambda b, j: (row_blk(b, j), 0)),
            pl.BlockSpec((POOL_HALO, gw),
                         lambda b, j: (jnp.maximum(row_blk(b, j) * halo_per_blk - 1, 0), 2)),
            pl.BlockSpec((rows, 3 * gw), lambda b, j: (row_blk(b, j), 0)),
            pl.BlockSpec((CHUNK, gw),
                         lambda b, j: (jnp.maximum(row_blk(b, j) * kv_per_blk - 1, 0), 1)),
            pl.BlockSpec((CHUNK, gw),
                         lambda b, j: (jnp.maximum(row_blk(b, j) * kv_per_blk - 1, 0), 2)),
            whole, whole, whole, whole, whole, whole, whole,
        ],
        out_specs=pl.BlockSpec((rows, 3 * gw), lambda b, j: (row_blk(b, j), 0)),
        out_shape=jax.ShapeDtypeStruct((m, 3 * gw), BF16),
        scratch_shapes=[pltpu.VMEM((n_heads * CHUNK, 2 * CHUNK), F32),
                        pltpu.VMEM((CHUNK, n_heads * CHUNK), BF16)],
        compiler_params=pltpu.CompilerParams(
            dimension_semantics=("arbitrary", "arbitrary"), vmem_limit_bytes=VMEM_LIMIT_MIX),
        name="local_mix",
    )(rel_bias, sinks, pa, pa, pc, pc, pc, sgu_w, sgu_b_tile, avg, pool_bd, pool_scale,
      bucket, gain)


def _softplus(z):
    neg_abs = lax.bitcast_convert_type(
        lax.bitcast_convert_type(z, jnp.uint32) | jnp.uint32(0x80000000), F32)
    return jnp.maximum(z, 0.0) + jnp.log(1.0 + jnp.exp(neg_abs))


def _sb_attn_kernel(q_ref, k_ref, v_ref, upper_ref, gain_ref, o_ref,
                    qm_ref, acc_ref, carry_ref, cmin_ref):
    m = pl.program_id(1)
    n_sub = SB_SUBBLOCKS
    blk = q_ref.shape[0] // n_sub
    gw = q_ref.shape[1]
    half = gw // 2
    n_heads = gw // HEAD_DIM
    rows = n_heads * blk
    lane = lax.broadcasted_iota(jnp.int32, (1, gw), 1)
    lane_h = lax.broadcasted_iota(jnp.int32, (1, half), 1)

    for s in range(n_sub):
        q = q_ref[s * blk:(s + 1) * blk, :]
        for h in range(n_heads):
            qm_ref[s, h * blk:(h + 1) * blk, :] = jnp.where(lane // HEAD_DIM == h, q,
                                                            jnp.zeros_like(q))
    upper = upper_ref[...]
    causal = (lax.broadcasted_iota(jnp.int32, (blk, blk), 1)
              < lax.broadcasted_iota(jnp.int32, (blk, blk), 0))

    def keep_causal(x):
        return jnp.where(causal, x.reshape(n_heads, blk, blk), 0.0).reshape(rows, blk)

    def tile(s, t, carry, diag):
        start = pl.multiple_of((n_sub * m + s - t) * blk, blk)
        z = _dot_nt(qm_ref[s], k_ref[pl.ds(start, blk), :])
        sp = _softplus(z)
        log_sig = z - sp
        if diag:
            sp = keep_causal(sp)
        tail = _dot(sp.astype(BF16), upper)
        if carry is not None:
            tail = tail + jnp.concatenate([carry, carry], axis=1)
        w = jnp.exp(log_sig - tail)
        if diag:
            w = keep_causal(w)
        w = w.astype(BF16)
        rowsum = jnp.broadcast_to(jnp.sum(sp, axis=-1, keepdims=True), (rows, half))
        pv = []
        for p in range(n_heads // 2):
            r = _dot(w[2 * p * blk:(2 * p + 2) * blk],
                     v_ref[pl.ds(start, blk), p * half:(p + 1) * half])
            pv.append(jnp.where(lane_h < HEAD_DIM, r[:blk], r[blk:]))
        return pv, rowsum if carry is None else carry + rowsum

    def first_tiles(first_step):
        for s in range(n_sub):
            pv, carry = tile(s, 0, None, True)
            if s > 0 or not first_step:
                pv_b, carry = tile(s, 1, carry, False)
                pv = [a + b for a, b in zip(pv, pv_b)]
            carry_ref[s] = carry
            for p in range(n_heads // 2):
                acc_ref[s, p] = pv[p]
            cmin_ref[s] = jnp.min(carry)

    pl.when(m == 0)(functools.partial(first_tiles, True))
    pl.when(m > 0)(functools.partial(first_tiles, False))

    for s in range(n_sub):
        def more(state, s=s):
            t, cmin = state
            return jnp.logical_and(t <= n_sub * m + s, cmin <= EXP_UNDERFLOW)

        def visit(state, s=s):
            t, _ = state
            pv, carry = tile(s, t, carry_ref[s], False)
            carry_ref[s] = carry
            for p in range(n_heads // 2):
                acc_ref[s, p] += pv[p]
            return t + 1, jnp.min(carry)

        lax.while_loop(more, visit, (2, cmin_ref[s]))

    for s in range(n_sub):
        y = jnp.concatenate([acc_ref[s, p] for p in range(n_heads // 2)], axis=-1)
        o_ref[s * blk:(s + 1) * blk, :] = _rms(y, gain_ref[...]).astype(o_ref.dtype)


def _sb_attn(pd, upper, gain, batch, seq):
    m = pd.shape[0]
    gw = GROUP_WIDTH
    blk = SB_BLOCK
    n_sub = SB_SUBBLOCKS
    nq = seq // (n_sub * blk)
    n_heads = gw // HEAD_DIM
    whole = pl.BlockSpec(memory_space=pltpu.VMEM)
    return pl.pallas_call(
        _sb_attn_kernel,
        grid=(batch, nq),
        in_specs=[
            pl.BlockSpec((n_sub * blk, gw), lambda b, j: (b * nq + j, 0)),
            pl.BlockSpec((seq, gw), lambda b, j: (b, 1)),
            pl.BlockSpec((seq, gw), lambda b, j: (b, 2)),
            whole, whole,
        ],
        out_specs=pl.BlockSpec((n_sub * blk, gw), lambda b, j: (b * nq + j, 0)),
        out_shape=jax.ShapeDtypeStruct((m, gw), BF16),
        scratch_shapes=[pltpu.VMEM((n_sub, n_heads * blk, gw), BF16),
                        pltpu.VMEM((n_sub, n_heads // 2, blk, gw // 2), F32),
                        pltpu.VMEM((n_sub, n_heads * blk, gw // 2), F32),
                        pltpu.SMEM((n_sub,), F32)],
        compiler_params=pltpu.CompilerParams(
            dimension_semantics=("arbitrary", "arbitrary"), vmem_limit_bytes=VMEM_LIMIT_MIX),
        name="sb_attn",
    )(pd, pd, pd, upper, gain)


def _out_ffn_kernel(x_ref, yabc_ref, yd_ref, wout_ref, g_ref, wgu_ref, wdown_ref, gfin_ref,
                    o_ref, x2_ref, act_ref, *, final_norm):
    n_abc = yabc_ref.shape[1]
    d_ff = wdown_ref.shape[0]
    x2_ref[...] = (x_ref[...]
                   + _dot(yabc_ref[...], wout_ref[0:n_abc, :])
                   + _dot(yd_ref[...], wout_ref[n_abc:, :]))
    h = _rms(x2_ref[...], g_ref[...]).astype(BF16)
    for c in range(d_ff // FF_CHUNK):
        c0 = c * FF_CHUNK
        gate = _dot(h, wgu_ref[:, c0:c0 + FF_CHUNK])
        up = _dot(h, wgu_ref[:, d_ff + c0:d_ff + c0 + FF_CHUNK])
        act_ref[:, c0:c0 + FF_CHUNK] = (gate * (1.0 / (1.0 + jnp.exp(-gate))) * up).astype(BF16)
    out = x2_ref[...] + _dot(act_ref[...], wdown_ref[...])
    if final_norm:
        out = _rms(out, gfin_ref[...])
    o_ref[...] = out


def _out_ffn(x2d, yabc, yd, w_out, g, w_gu, w_down, g_final, layer, final_norm):
    m, d = x2d.shape
    tm = FFN_ROWS
    vec = pl.BlockSpec((1, d), lambda i: (0, 0))
    return pl.pallas_call(
        functools.partial(_out_ffn_kernel, final_norm=final_norm),
        grid=(m // tm,),
        in_specs=[
            pl.BlockSpec((tm, d), lambda i: (i, 0)),
            pl.BlockSpec((tm, yabc.shape[1]), lambda i: (i, 0)),
            pl.BlockSpec((tm, yd.shape[1]), lambda i: (i, 0)),
            _layer_weight(w_out, layer), vec,
            _layer_weight(w_gu, layer), _layer_weight(w_down, layer), vec,
        ],
        out_specs=pl.BlockSpec((tm, d), lambda i: (i, 0)),
        out_shape=jax.ShapeDtypeStruct((m, d), F32),
        scratch_shapes=[pltpu.VMEM((tm, d), F32), pltpu.VMEM((tm, w_down.shape[1]), BF16)],
        compiler_params=pltpu.CompilerParams(
            dimension_semantics=("arbitrary",), vmem_limit_bytes=VMEM_LIMIT_DENSE),
        name="out_ffn",
    )(x2d, yabc, yd, w_out, g, w_gu, w_down, g_final)


def _t5_bucket(dist):
    max_exact = N_BUCKETS // 2
    df = jnp.maximum(dist, 1).astype(F32)
    large = max_exact + (jnp.log(df / max_exact) / math.log(MAX_DISTANCE / max_exact)
                         * (N_BUCKETS - max_exact)).astype(jnp.int32)
    large = jnp.minimum(large, N_BUCKETS - 1)
    return jnp.where(dist < max_exact, dist, large)


def kernel(x, w_in, w_out, sgu_w, sgu_b, pool_w, pool_scale, swa_sinks, rel_bias, mix_out_gain,
           norm_mix, norm_ffn, w_gate_up, w_down, norm_final):
    batch, seq, d = x.shape
    depth = w_in.shape[0]
    gw = GROUP_WIDTH
    n_pool = len(POOL_WINDOWS)
    rows = batch * seq
    assert seq % LOCAL_ROWS == 0 and seq % (SB_BLOCK * SB_SUBBLOCKS) == 0
    assert rows % PROJ_ROWS == 0 and rows % FFN_ROWS == 0

    head_id = np.arange(gw) // HEAD_DIM
    avg = (head_id[:, None] == head_id[None, :]) / HEAD_DIM
    avg2 = jnp.asarray(np.concatenate([avg, avg], axis=0), BF16)
    upper = jnp.asarray(np.arange(SB_BLOCK)[:, None] > np.arange(SB_BLOCK)[None, :], BF16)
    dist = (jnp.arange(CHUNK)[:, None] + CHUNK) - jnp.arange(2 * CHUNK)[None, :]
    bucket = _t5_bucket(jnp.clip(dist, 0, CHUNK - 1)).astype(jnp.int32)

    w_in_b, w_out_b = w_in.astype(BF16), w_out.astype(BF16)
    w_gu_b, w_down_b = w_gate_up.astype(BF16), w_down.astype(BF16)
    pool_bd = (pool_w[:, :, :, None, :] * jnp.eye(n_pool, dtype=F32)[None, :, None, :, None])
    pool_bd = pool_bd.reshape(depth, gw, gw).astype(BF16)
    sgu_b_tile = jnp.repeat(jnp.swapaxes(sgu_b, 1, 2), HEAD_DIM, axis=2)
    gains = mix_out_gain.reshape(depth, 1, 4 * gw)

    x2d = x.reshape(rows, d)
    for l in range(depth):
        pa, pc, pd = _norm_proj(x2d, norm_mix[l].reshape(1, d), w_in_b, l)
        yabc = _local_mix(pa, pc, rel_bias, swa_sinks[l], sgu_w[l], sgu_b_tile[l], avg2,
                          pool_bd[l], pool_scale[l].reshape(1, gw), bucket,
                          gains[l, :, :3 * gw], batch, seq)
        yd = _sb_attn(pd, upper, gains[l, :, 3 * gw:], batch, seq)
        x2d = _out_ffn(x2d, yabc, yd, w_out_b, norm_ffn[l].reshape(1, d), w_gu_b, w_down_b,
                       norm_final.reshape(1, d), l, l == depth - 1)
    return x2d.reshape(batch, seq, d)
```

```python
import functools
import math

import numpy as np
import jax
import jax.numpy as jnp
from jax import lax
from jax.experimental import pallas as pl
from jax.experimental.pallas import tpu as pltpu

F32 = jnp.float32
BF16 = jnp.bfloat16

HEAD_DIM = 64
GROUP_WIDTH = 256
CHUNK = 128
POOL_WINDOWS = (2, 4, 8, 16)
POOL_HALO = 16
N_BUCKETS = 32
MAX_DISTANCE = 128
EPS = 1e-6
MASK_VALUE = -1e30
EXP_UNDERFLOW = 104.0

LOCAL_ROWS = 512
SB_BLOCK = 256
SB_SUBBLOCKS = 8
PROJ_ROWS = 512
FFN_ROWS = 512
FF_CHUNK = 256
VMEM_LIMIT_DENSE = 48 * 1024 * 1024
VMEM_LIMIT_MIX = 40 * 1024 * 1024


def _rms(x, g):
    return x * lax.rsqrt(jnp.mean(x * x, axis=-1, keepdims=True) + EPS) * g


def _gelu_tanh(x):
    inner = math.sqrt(2.0 / math.pi) * (x + 0.044715 * (x * x * x))
    return x * (0.5 * (1.0 + jnp.tanh(inner)))


def _split_bf16(x):
    hi = x.astype(BF16)
    lo = (x - hi.astype(F32)).astype(BF16)
    return hi, lo


def _dot(a, b):
    return jnp.dot(a, b, preferred_element_type=F32)


def _dot_nt(a, b):
    return lax.dot_general(a, b, (((1,), (1,)), ((), ())), preferred_element_type=F32)


def _layer_weight(w, layer):
    return pl.BlockSpec((None,) + w.shape[1:], lambda *_: (layer, 0, 0),
                        pipeline_mode=pl.Buffered(1))


def _per_query_head(kv):
    lane = lax.broadcasted_iota(jnp.int32, (1, kv.shape[1]), 1)
    swapped = pltpu.roll(kv, HEAD_DIM, 1)
    first = jnp.where(lane < HEAD_DIM, kv, swapped)
    second = jnp.where(lane < HEAD_DIM, swapped, kv)
    return jnp.concatenate([first, second], axis=1)


def _norm_proj_kernel(x_ref, g_ref, w_ref, pa_ref, pc_ref, pd_ref):
    gw = GROUP_WIDTH
    kvw = w_ref.shape[1] - 7 * gw
    scale = HEAD_DIM ** -0.5
    h = _rms(x_ref[...], g_ref[...]).astype(BF16)
    y = _dot(h, w_ref[...])
    pa_ref[...] = y[:, :3 * gw]
    c_k = 4 * gw
    c_v = c_k + kvw // 2
    d_q = c_v + kvw // 2
    pc_ref[:, 0:gw] = (y[:, 3 * gw:4 * gw] * scale).astype(BF16)
    pc_ref[:, gw:2 * gw] = _per_query_head(y[:, c_k:c_v]).astype(BF16)
    pc_ref[:, 2 * gw:3 * gw] = _per_query_head(y[:, c_v:d_q]).astype(BF16)
    pd_ref[:, 0:gw] = (y[:, d_q:d_q + gw] * scale).astype(BF16)
    pd_ref[:, gw:3 * gw] = y[:, d_q + gw:].astype(BF16)


def _norm_proj(x2d, g, w, layer):
    m, d = x2d.shape
    tm = PROJ_ROWS
    na = nc = nd = 3 * GROUP_WIDTH
    assert w.shape[2] == 7 * GROUP_WIDTH + 4 * HEAD_DIM, "expects 2 kv heads of HEAD_DIM"
    return pl.pallas_call(
        _norm_proj_kernel,
        grid=(m // tm,),
        in_specs=[
            pl.BlockSpec((tm, d), lambda i: (i, 0)),
            pl.BlockSpec((1, d), lambda i: (0, 0)),
            _layer_weight(w, layer),
        ],
        out_specs=[
            pl.BlockSpec((tm, na), lambda i: (i, 0)),
            pl.BlockSpec((tm, nc), lambda i: (i, 0)),
            pl.BlockSpec((tm, nd), lambda i: (i, 0)),
        ],
        out_shape=[
            jax.ShapeDtypeStruct((m, na), F32),
            jax.ShapeDtypeStruct((m, nc), BF16),
            jax.ShapeDtypeStruct((m, nd), BF16),
        ],
        compiler_params=pltpu.CompilerParams(
            dimension_semantics=("arbitrary",), vmem_limit_bytes=VMEM_LIMIT_MIX),
        name="norm_proj",
    )(x2d, g, w)


def _mask_heads(x, n_heads):
    lane = lax.broadcasted_iota(jnp.int32, (1, x.shape[1]), 1)
    return jnp.concatenate(
        [jnp.where(lane // HEAD_DIM == h, x, jnp.zeros_like(x)) for h in range(n_heads)], axis=0)


def _local_mix_kernel(relb_ref, sink_ref,
                      pa_ref, halo_ref, pc_ref, kh_ref, vh_ref,
                      sguw_ref, sgub_ref, avg_ref, poolw_ref, pscale_ref,
                      bucket_ref, gain_ref,
                      o_ref,
                      bias_ref, wcat_ref):
    b = pl.program_id(0)
    j = pl.program_id(1)
    rows = pa_ref.shape[0]
    gw = GROUP_WIDTH
    n_heads = gw // HEAD_DIM
    n_chunks = rows // CHUNK
    lane = lax.broadcasted_iota(jnp.int32, (1, gw), 1)

    @pl.when(jnp.logical_and(b == 0, j == 0))
    def _():
        bucket = bucket_ref[...]
        tril = (lax.broadcasted_iota(jnp.int32, (CHUNK, CHUNK), 1)
                <= lax.broadcasted_iota(jnp.int32, (CHUNK, CHUNK), 0))
        for h in range(n_heads):
            acc = jnp.zeros(bucket.shape, F32)
            for bk in range(N_BUCKETS):
                acc = jnp.where(bucket == bk, relb_ref[bk, h], acc)
            bias_ref[h * CHUNK:(h + 1) * CHUNK, :] = acc
            wcat_ref[:, h * CHUNK:(h + 1) * CHUNK] = jnp.where(tril, sguw_ref[h], 0.0).astype(BF16)

    gain = gain_ref[...]

    p = pa_ref[:, 2 * gw:3 * gw]
    halo = jnp.where(j == 0, 0.0, halo_ref[...])
    ext = jnp.concatenate([halo, p], axis=0)
    sums = []
    run = ext
    for shift in (1, 2, 4, 8):
        run = run + pltpu.roll(run, shift, 0)
        sums.append(run[POOL_HALO:])
    group = lane // (gw // len(POOL_WINDOWS))
    wsum = jnp.where(group == 0, sums[0],
                     jnp.where(group == 1, sums[1],
                               jnp.where(group == 2, sums[2], sums[3])))
    win = jnp.where(group == 0, POOL_WINDOWS[0],
                    jnp.where(group == 1, POOL_WINDOWS[1],
                              jnp.where(group == 2, POOL_WINDOWS[2], POOL_WINDOWS[3])))
    t_seq = j * rows + lax.broadcasted_iota(jnp.int32, (rows, 1), 0)
    count = jnp.minimum(t_seq + 1, win).astype(F32)
    yb = wsum / count - p
    yb = _dot(yb.astype(BF16), poolw_ref[...]) * pscale_ref[...]
    o_ref[:, gw:2 * gw] = _rms(yb, gain[:, gw:2 * gw]).astype(o_ref.dtype)

    avg = avg_ref[...]
    u = _gelu_tanh(pa_ref[:, 0:gw])
    v = _gelu_tanh(pa_ref[:, gw:2 * gw])
    v_hi, v_lo = _split_bf16(v)
    vc = v - _dot(jnp.concatenate([v_hi, v_lo], axis=1), avg)
    sq_hi, sq_lo = _split_bf16(vc * vc)
    var = _dot(jnp.concatenate([sq_hi, sq_lo], axis=1), avg)
    vn = (vc * lax.rsqrt(var + EPS)).astype(BF16)
    rhs = jnp.concatenate(
        [_mask_heads(vn[c * CHUNK:(c + 1) * CHUNK], n_heads) for c in range(n_chunks)], axis=1)
    mix = _dot(wcat_ref[...], rhs)
    for c in range(n_chunks):
        r0 = c * CHUNK
        ya = u[r0:r0 + CHUNK] * (mix[:, c * gw:(c + 1) * gw] + sgub_ref[...])
        o_ref[r0:r0 + CHUNK, 0:gw] = _rms(ya, gain[:, 0:gw]).astype(o_ref.dtype)

    stacked = (n_heads * CHUNK, 2 * CHUNK)
    qi = lax.broadcasted_iota(jnp.int32, stacked, 0) & (CHUNK - 1)
    ki = lax.broadcasted_iota(jnp.int32, stacked, 1)
    in_window = jnp.logical_and(ki > qi, ki <= qi + CHUNK)
    head_of_row = lax.broadcasted_iota(jnp.int32, (n_heads * CHUNK, 1), 0) // CHUNK
    sink = jnp.zeros((n_heads * CHUNK, 1), F32)
    for h in range(n_heads):
        sink = jnp.where(head_of_row == h, sink_ref[h], sink)
    bias = bias_ref[...]
    for c in range(n_chunks):
        r0 = c * CHUNK
        if c == 0:
            k_prev, v_prev = kh_ref[...], vh_ref[...]
            mask = jnp.logical_and(in_window, jnp.logical_or(ki >= CHUNK, j > 0))
        else:
            k_prev = pc_ref[r0 - CHUNK:r0, gw:2 * gw]
            v_prev = pc_ref[r0 - CHUNK:r0, 2 * gw:3 * gw]
            mask = in_window
        k2 = jnp.concatenate([k_prev, pc_ref[r0:r0 + CHUNK, gw:2 * gw]], axis=0)
        v2 = jnp.concatenate([v_prev, pc_ref[r0:r0 + CHUNK, 2 * gw:3 * gw]], axis=0)
        qm = _mask_heads(pc_ref[r0:r0 + CHUNK, 0:gw], n_heads)
        logits = jnp.where(mask, _dot_nt(qm, k2) + bias, MASK_VALUE)
        mx = jnp.maximum(jnp.max(logits, axis=-1, keepdims=True), sink)
        e = jnp.exp(logits - mx)
        denom = jnp.sum(e, axis=-1, keepdims=True) + jnp.exp(sink - mx)
        probs = (e * (1.0 / denom)).astype(BF16)
        pcat = jnp.concatenate([probs[h * CHUNK:(h + 1) * CHUNK] for h in range(n_heads)], axis=1)
        yc = _dot(pcat, _mask_heads(v2, n_heads))
        o_ref[r0:r0 + CHUNK, 2 * gw:3 * gw] = _rms(yc, gain[:, 2 * gw:3 * gw]).astype(o_ref.dtype)


def _local_mix(pa, pc, rel_bias, sinks, sgu_w, sgu_b_tile, avg, pool_bd, pool_scale,
               bucket, gain, batch, seq):
    m = pa.shape[0]
    rows = LOCAL_ROWS
    nq = seq // rows
    gw = GROUP_WIDTH
    n_heads = gw // HEAD_DIM

    def row_blk(b, j):
        return b * nq + j

    halo_per_blk = rows // POOL_HALO
    kv_per_blk = rows // CHUNK
    smem = pl.BlockSpec(memory_space=pltpu.SMEM)
    whole = pl.BlockSpec(memory_space=pltpu.VMEM)
    return pl.pallas_call(
        _local_mix_kernel,
        grid=(batch, nq),
        in_specs=[
            smem, smem,
            pl.BlockSpec((rows, 3 * gw), lambda b, j: (row_blk(b, j), 0)),
            pl.BlockSpec((POOL_HALO, gw),
                         lambda b, j: (jnp.maximum(row_blk(b, j) * halo_per_blk - 1, 0), 2)),
            pl.BlockSpec((rows, 3 * gw), lambda b, j: (row_blk(b, j), 0)),
            pl.BlockSpec((CHUNK, gw),
                         lambda b, j: (jnp.maximum(row_blk(b, j) * kv_per_blk - 1, 0), 1)),
            pl.BlockSpec((CHUNK, gw),
                         lambda b, j: (jnp.maximum(row_blk(b, j) * kv_per_blk - 1, 0), 2)),
            whole, whole, whole, whole, whole, whole, whole,
        ],
        out_specs=pl.BlockSpec((rows, 3 * gw), lambda b, j: (row_blk(b, j), 0)),
        out_shape=jax.ShapeDtypeStruct((m, 3 * gw), BF16),
        scratch_shapes=[pltpu.VMEM((n_heads * CHUNK, 2 * CHUNK), F32),
                        pltpu.VMEM((CHUNK, n_heads * CHUNK), BF16)],
        compiler_params=pltpu.CompilerParams(
            dimension_semantics=("arbitrary", "arbitrary"), vmem_limit_bytes=VMEM_LIMIT_MIX),
        name="local_mix",
    )(rel_bias, sinks, pa, pa, pc, pc, pc, sgu_w, sgu_b_tile, avg, pool_bd, pool_scale,
      bucket, gain)


def _softplus(z):
    neg_abs = lax.bitcast_convert_type(
        lax.bitcast_convert_type(z, jnp.uint32) | jnp.uint32(0x80000000), F32)
    return jnp.maximum(z, 0.0) + jnp.log(1.0 + jnp.exp(neg_abs))


def _sb_attn_kernel(q_ref, k_ref, v_ref, upper_ref, gain_ref, o_ref,
                    qm_ref, acc_ref, carry_ref, cmin_ref):
    m = pl.program_id(1)
    n_sub = SB_SUBBLOCKS
    blk = q_ref.shape[0] // n_sub
    gw = q_ref.shape[1]
    half = gw // 2
    n_heads = gw // HEAD_DIM
    rows = n_heads * blk
    lane = lax.broadcasted_iota(jnp.int32, (1, gw), 1)
    lane_h = lax.broadcasted_iota(jnp.int32, (1, half), 1)

    for s in range(n_sub):
        q = q_ref[s * blk:(s + 1) * blk, :]
        for h in range(n_heads):
            qm_ref[s, h * blk:(h + 1) * blk, :] = jnp.where(lane // HEAD_DIM == h, q,
                                                            jnp.zeros_like(q))
    upper = upper_ref[...]
    causal = (lax.broadcasted_iota(jnp.int32, (blk, blk), 1)
              < lax.broadcasted_iota(jnp.int32, (blk, blk), 0))

    def keep_causal(x):
        return jnp.where(causal, x.reshape(n_heads, blk, blk), 0.0).reshape(rows, blk)

    def tile(s, t, carry, diag):
        start = pl.multiple_of((n_sub * m + s - t) * blk, blk)
        z = _dot_nt(qm_ref[s], k_ref[pl.ds(start, blk), :])
        sp = _softplus(z)
        log_sig = z - sp
        if diag:
            sp = keep_causal(sp)
        tail = _dot(sp.astype(BF16), upper)
        if carry is not None:
            tail = tail + jnp.concatenate([carry, carry], axis=1)
        w = jnp.exp(log_sig - tail)
        if diag:
            w = keep_causal(w)
        w = w.astype(BF16)
        rowsum = jnp.broadcast_to(jnp.sum(sp, axis=-1, keepdims=True), (rows, half))
        pv = []
        for p in range(n_heads // 2):
            r = _dot(w[2 * p * blk:(2 * p + 2) * blk],
                     v_ref[pl.ds(start, blk), p * half:(p + 1) * half])
            pv.append(jnp.where(lane_h < HEAD_DIM, r[:blk], r[blk:]))
        return pv, rowsum if carry is None else carry + rowsum

    def first_tiles(first_step):
        for s in range(n_sub):
            pv, carry = tile(s, 0, None, True)
            if s > 0 or not first_step:
                pv_b, carry = tile(s, 1, carry, False)
                pv = [a + b for a, b in zip(pv, pv_b)]
            carry_ref[s] = carry
            for p in range(n_heads // 2):
                acc_ref[s, p] = pv[p]
            cmin_ref[s] = jnp.min(carry)

    pl.when(m == 0)(functools.partial(first_tiles, True))
    pl.when(m > 0)(functools.partial(first_tiles, False))

    for s in range(n_sub):
        def more(state, s=s):
            t, cmin = state
            return jnp.logical_and(t <= n_sub * m + s, cmin <= EXP_UNDERFLOW)

        def visit(state, s=s):
            t, _ = state
            pv, carry = tile(s, t, carry_ref[s], False)
            carry_ref[s] = carry
            for p in range(n_heads // 2):
                acc_ref[s, p] += pv[p]
            return t + 1, jnp.min(carry)

        lax.while_loop(more, visit, (2, cmin_ref[s]))

    for s in range(n_sub):
        y = jnp.concatenate([acc_ref[s, p] for p in range(n_heads // 2)], axis=-1)
        o_ref[s * blk:(s + 1) * blk, :] = _rms(y, gain_ref[...]).astype(o_ref.dtype)


def _sb_attn(pd, upper, gain, batch, seq):
    m = pd.shape[0]
    gw = GROUP_WIDTH
    blk = SB_BLOCK
    n_sub = SB_SUBBLOCKS
    nq = seq // (n_sub * blk)
    n_heads = gw // HEAD_DIM
    whole = pl.BlockSpec(memory_space=pltpu.VMEM)
    return pl.pallas_call(
        _sb_attn_kernel,
        grid=(batch, nq),
        in_specs=[
            pl.BlockSpec((n_sub * blk, gw), lambda b, j: (b * nq + j, 0)),
            pl.BlockSpec((seq, gw), lambda b, j: (b, 1)),
            pl.BlockSpec((seq, gw), lambda b, j: (b, 2)),
            whole, whole,
        ],
        out_specs=pl.BlockSpec((n_sub * blk, gw), lambda b, j: (b * nq + j, 0)),
        out_shape=jax.ShapeDtypeStruct((m, gw), BF16),
        scratch_shapes=[pltpu.VMEM((n_sub, n_heads * blk, gw), BF16),
                        pltpu.VMEM((n_sub, n_heads // 2, blk, gw // 2), F32),
                        pltpu.VMEM((n_sub, n_heads * blk, gw // 2), F32),
                        pltpu.SMEM((n_sub,), F32)],
        compiler_params=pltpu.CompilerParams(
            dimension_semantics=("arbitrary", "arbitrary"), vmem_limit_bytes=VMEM_LIMIT_MIX),
        name="sb_attn",
    )(pd, pd, pd, upper, gain)


def _out_ffn_kernel(x_ref, yabc_ref, yd_ref, wout_ref, g_ref, wgu_ref, wdown_ref, gfin_ref,
                    o_ref, x2_ref, act_ref, *, final_norm):
    n_abc = yabc_ref.shape[1]
    d_ff = wdown_ref.shape[0]
    x2_ref[...] = (x_ref[...]
                   + _dot(yabc_ref[...], wout_ref[0:n_abc, :])
                   + _dot(yd_ref[...], wout_ref[n_abc:, :]))
    h = _rms(x2_ref[...], g_ref[...]).astype(BF16)
    for c in range(d_ff // FF_CHUNK):
        c0 = c * FF_CHUNK
        gate = _dot(h, wgu_ref[:, c0:c0 + FF_CHUNK])
        up = _dot(h, wgu_ref[:, d_ff + c0:d_ff + c0 + FF_CHUNK])
        act_ref[:, c0:c0 + FF_CHUNK] = (gate * (1.0 / (1.0 + jnp.exp(-gate))) * up).astype(BF16)
    out = x2_ref[...] + _dot(act_ref[...], wdown_ref[...])
    if final_norm:
        out = _rms(out, gfin_ref[...])
    o_ref[...] = out


def _out_ffn(x2d, yabc, yd, w_out, g, w_gu, w_down, g_final, layer, final_norm):
    m, d = x2d.shape
    tm = FFN_ROWS
    vec = pl.BlockSpec((1, d), lambda i: (0, 0))
    return pl.pallas_call(
        functools.partial(_out_ffn_kernel, final_norm=final_norm),
        grid=(m // tm,),
        in_specs=[
            pl.BlockSpec((tm, d), lambda i: (i, 0)),
            pl.BlockSpec((tm, yabc.shape[1]), lambda i: (i, 0)),
            pl.BlockSpec((tm, yd.shape[1]), lambda i: (i, 0)),
            _layer_weight(w_out, layer), vec,
            _layer_weight(w_gu, layer), _layer_weight(w_down, layer), vec,
        ],
        out_specs=pl.BlockSpec((tm, d), lambda i: (i, 0)),
        out_shape=jax.ShapeDtypeStruct((m, d), F32),
        scratch_shapes=[pltpu.VMEM((tm, d), F32), pltpu.VMEM((tm, w_down.shape[1]), BF16)],
        compiler_params=pltpu.CompilerParams(
            dimension_semantics=("arbitrary",), vmem_limit_bytes=VMEM_LIMIT_DENSE),
        name="out_ffn",
    )(x2d, yabc, yd, w_out, g, w_gu, w_down, g_final)


def _t5_bucket(dist):
    max_exact = N_BUCKETS // 2
    df = jnp.maximum(dist, 1).astype(F32)
    large = max_exact + (jnp.log(df / max_exact) / math.log(MAX_DISTANCE / max_exact)
                         * (N_BUCKETS - max_exact)).astype(jnp.int32)
    large = jnp.minimum(large, N_BUCKETS - 1)
    return jnp.where(dist < max_exact, dist, large)


def kernel(x, w_in, w_out, sgu_w, sgu_b, pool_w, pool_scale, swa_sinks, rel_bias, mix_out_gain,
           norm_mix, norm_ffn, w_gate_up, w_down, norm_final):
    batch, seq, d = x.shape
    depth = w_in.shape[0]
    gw = GROUP_WIDTH
    n_pool = len(POOL_WINDOWS)
    rows = batch * seq
    assert seq % LOCAL_ROWS == 0 and seq % (SB_BLOCK * SB_SUBBLOCKS) == 0
    assert rows % PROJ_ROWS == 0 and rows % FFN_ROWS == 0

    head_id = np.arange(gw) // HEAD_DIM
    avg = (head_id[:, None] == head_id[None, :]) / HEAD_DIM
    avg2 = jnp.asarray(np.concatenate([avg, avg], axis=0), BF16)
    upper = jnp.asarray(np.arange(SB_BLOCK)[:, None] > np.arange(SB_BLOCK)[None, :], BF16)
    dist = (jnp.arange(CHUNK)[:, None] + CHUNK) - jnp.arange(2 * CHUNK)[None, :]
    bucket = _t5_bucket(jnp.clip(dist, 0, CHUNK - 1)).astype(jnp.int32)

    w_in_b, w_out_b = w_in.astype(BF16), w_out.astype(BF16)
    w_gu_b, w_down_b = w_gate_up.astype(BF16), w_down.astype(BF16)
    pool_bd = (pool_w[:, :, :, None, :] * jnp.eye(n_pool, dtype=F32)[None, :, None, :, None])
    pool_bd = pool_bd.reshape(depth, gw, gw).astype(BF16)
    sgu_b_tile = jnp.repeat(jnp.swapaxes(sgu_b, 1, 2), HEAD_DIM, axis=2)
    gains = mix_out_gain.reshape(depth, 1, 4 * gw)

    x2d = x.reshape(rows, d)
    for l in range(depth):
        pa, pc, pd = _norm_proj(x2d, norm_mix[l].reshape(1, d), w_in_b, l)
        yabc = _local_mix(pa, pc, rel_bias, swa_sinks[l], sgu_w[l], sgu_b_tile[l], avg2,
                          pool_bd[l], pool_scale[l].reshape(1, gw), bucket,
                          gains[l, :, :3 * gw], batch, seq)
        yd = _sb_attn(pd, upper, gains[l, :, 3 * gw:], batch, seq)
        x2d = _out_ffn(x2d, yabc, yd, w_out_b, norm_ffn[l].reshape(1, d), w_gu_b, w_down_b,
                       norm_final.reshape(1, d), l, l == depth - 1)
    return x2d.reshape(batch, seq, d)
```

```python
import functools
import math

import numpy as np
import jax
import jax.numpy as jnp
from jax import lax
from jax.experimental import pallas as pl
from jax.experimental.pallas import tpu as pltpu

F32 = jnp.float32
BF16 = jnp.bfloat16

HEAD_DIM = 64
GROUP_WIDTH = 256
CHUNK = 128
POOL_WINDOWS = (2, 4, 8, 16)
POOL_HALO = 16
N_BUCKETS = 32
MAX_DISTANCE = 128
EPS = 1e-6
MASK_VALUE = -1e30
EXP_UNDERFLOW = 104.0

LOCAL_ROWS = 1024
SB_BLOCK = 256
SB_SUBBLOCKS = 8
PROJ_ROWS = 512
FFN_ROWS = 512
FF_CHUNK = 256
VMEM_LIMIT_DENSE = 48 * 1024 * 1024
VMEM_LIMIT_MIX = 40 * 1024 * 1024


def _rms(x, g):
    return x * lax.rsqrt(jnp.mean(x * x, axis=-1, keepdims=True) + EPS) * g


def _gelu_tanh(x):
    inner = math.sqrt(2.0 / math.pi) * (x + 0.044715 * (x * x * x))
    return x * (0.5 * (1.0 + jnp.tanh(inner)))


def _split_bf16(x):
    hi = x.astype(BF16)
    lo = (x - hi.astype(F32)).astype(BF16)
    return hi, lo


def _dot(a, b):
    return jnp.dot(a, b, preferred_element_type=F32)


def _dot_nt(a, b):
    return lax.dot_general(a, b, (((1,), (1,)), ((), ())), preferred_element_type=F32)


def _layer_weight(w, layer):
    return pl.BlockSpec((None,) + w.shape[1:], lambda *_: (layer, 0, 0),
                        pipeline_mode=pl.Buffered(1))


def _per_query_head(kv):
    lane = lax.broadcasted_iota(jnp.int32, (1, kv.shape[1]), 1)
    swapped = pltpu.roll(kv, HEAD_DIM, 1)
    first = jnp.where(lane < HEAD_DIM, kv, swapped)
    second = jnp.where(lane < HEAD_DIM, swapped, kv)
    return jnp.concatenate([first, second], axis=1)


def _norm_proj_kernel(x_ref, g_ref, w_ref, pa_ref, pc_ref, pd_ref):
    gw = GROUP_WIDTH
    kvw = w_ref.shape[1] - 7 * gw
    scale = HEAD_DIM ** -0.5
    h = _rms(x_ref[...], g_ref[...]).astype(BF16)
    y = _dot(h, w_ref[...])
    pa_ref[...] = y[:, :3 * gw]
    c_k = 4 * gw
    c_v = c_k + kvw // 2
    d_q = c_v + kvw // 2
    pc_ref[:, 0:gw] = (y[:, 3 * gw:4 * gw] * scale).astype(BF16)
    pc_ref[:, gw:2 * gw] = _per_query_head(y[:, c_k:c_v]).astype(BF16)
    pc_ref[:, 2 * gw:3 * gw] = _per_query_head(y[:, c_v:d_q]).astype(BF16)
    pd_ref[:, 0:gw] = (y[:, d_q:d_q + gw] * scale).astype(BF16)
    pd_ref[:, gw:3 * gw] = y[:, d_q + gw:].astype(BF16)


def _norm_proj(x2d, g, w, layer):
    m, d = x2d.shape
    tm = PROJ_ROWS
    na = nc = nd = 3 * GROUP_WIDTH
    assert w.shape[2] == 7 * GROUP_WIDTH + 4 * HEAD_DIM, "expects 2 kv heads of HEAD_DIM"
    return pl.pallas_call(
        _norm_proj_kernel,
        grid=(m // tm,),
        in_specs=[
            pl.BlockSpec((tm, d), lambda i: (i, 0)),
            pl.BlockSpec((1, d), lambda i: (0, 0)),
            _layer_weight(w, layer),
        ],
        out_specs=[
            pl.BlockSpec((tm, na), lambda i: (i, 0)),
            pl.BlockSpec((tm, nc), lambda i: (i, 0)),
            pl.BlockSpec((tm, nd), lambda i: (i, 0)),
        ],
        out_shape=[
            jax.ShapeDtypeStruct((m, na), F32),
            jax.ShapeDtypeStruct((m, nc), BF16),
            jax.ShapeDtypeStruct((m, nd), BF16),
        ],
        compiler_params=pltpu.CompilerParams(
            dimension_semantics=("arbitrary",), vmem_limit_bytes=VMEM_LIMIT_MIX),
        name="norm_proj",
    )(x2d, g, w)


def _mask_heads(x, n_heads):
    lane = lax.broadcasted_iota(jnp.int32, (1, x.shape[1]), 1)
    return jnp.concatenate(
        [jnp.where(lane // HEAD_DIM == h, x, jnp.zeros_like(x)) for h in range(n_heads)], axis=0)


def _local_mix_kernel(relb_ref, sink_ref,
                      pa_ref, halo_ref, pc_ref, kh_ref, vh_ref,
                      sguw_ref, sgub_ref, avg_ref, poolw_ref, pscale_ref,
                      bucket_ref, gain_ref,
                      o_ref,
                      bias_ref, wcat_ref):
    b = pl.program_id(0)
    j = pl.program_id(1)
    rows = pa_ref.shape[0]
    gw = GROUP_WIDTH
    n_heads = gw // HEAD_DIM
    n_chunks = rows // CHUNK
    lane = lax.broadcasted_iota(jnp.int32, (1, gw), 1)

    @pl.when(jnp.logical_and(b == 0, j == 0))
    def _():
        bucket = bucket_ref[...]
        tril = (lax.broadcasted_iota(jnp.int32, (CHUNK, CHUNK), 1)
                <= lax.broadcasted_iota(jnp.int32, (CHUNK, CHUNK), 0))
        qi = lax.broadcasted_iota(jnp.int32, bucket.shape, 0)
        ki = lax.broadcasted_iota(jnp.int32, bucket.shape, 1)
        in_window = jnp.logical_and(ki > qi, ki <= qi + CHUNK)
        for h in range(n_heads):
            acc = jnp.zeros(bucket.shape, F32)
            for bk in range(N_BUCKETS):
                acc = jnp.where(bucket == bk, relb_ref[bk, h], acc)
            bias_ref[h * CHUNK:(h + 1) * CHUNK, :] = jnp.where(in_window, acc, MASK_VALUE)
            wcat_ref[:, h * CHUNK:(h + 1) * CHUNK] = jnp.where(tril, sguw_ref[h], 0.0).astype(BF16)

    gain = gain_ref[...]

    p = pa_ref[:, 2 * gw:3 * gw]
    halo = jnp.where(j == 0, 0.0, halo_ref[...])
    ext = jnp.concatenate([halo, p], axis=0)
    sums = []
    run = ext
    for shift in (1, 2, 4, 8):
        run = run + pltpu.roll(run, shift, 0)
        sums.append(run[POOL_HALO:])
    group = lane // (gw // len(POOL_WINDOWS))
    wsum = jnp.where(group == 0, sums[0],
                     jnp.where(group == 1, sums[1],
                               jnp.where(group == 2, sums[2], sums[3])))
    win = jnp.where(group == 0, POOL_WINDOWS[0],
                    jnp.where(group == 1, POOL_WINDOWS[1],
                              jnp.where(group == 2, POOL_WINDOWS[2], POOL_WINDOWS[3])))
    t_seq = j * rows + lax.broadcasted_iota(jnp.int32, (rows, 1), 0)
    count = jnp.minimum(t_seq + 1, win).astype(F32)
    yb = wsum / count - p
    yb = _dot(yb.astype(BF16), poolw_ref[...]) * pscale_ref[...]
    o_ref[:, gw:2 * gw] = _rms(yb, gain[:, gw:2 * gw]).astype(o_ref.dtype)

    avg = avg_ref[...]
    u = _gelu_tanh(pa_ref[:, 0:gw])
    v = _gelu_tanh(pa_ref[:, gw:2 * gw])
    v_hi, v_lo = _split_bf16(v)
    vc = v - _dot(jnp.concatenate([v_hi, v_lo], axis=1), avg)
    sq_hi, sq_lo = _split_bf16(vc * vc)
    var = _dot(jnp.concatenate([sq_hi, sq_lo], axis=1), avg)
    vn = (vc * lax.rsqrt(var + EPS)).astype(BF16)
    rhs = jnp.concatenate(
        [_mask_heads(vn[c * CHUNK:(c + 1) * CHUNK], n_heads) for c in range(n_chunks)], axis=1)
    mix = _dot(wcat_ref[...], rhs)
    for c in range(n_chunks):
        r0 = c * CHUNK
        ya = u[r0:r0 + CHUNK] * (mix[:, c * gw:(c + 1) * gw] + sgub_ref[...])
        o_ref[r0:r0 + CHUNK, 0:gw] = _rms(ya, gain[:, 0:gw]).astype(o_ref.dtype)

    ki = lax.broadcasted_iota(jnp.int32, (1, 2 * CHUNK), 1)
    head_of_row = lax.broadcasted_iota(jnp.int32, (n_heads * CHUNK, 1), 0) // CHUNK
    sink = jnp.zeros((n_heads * CHUNK, 1), F32)
    for h in range(n_heads):
        sink = jnp.where(head_of_row == h, sink_ref[h], sink)
    bias = bias_ref[...]
    for c in range(n_chunks):
        r0 = c * CHUNK
        if c == 0:
            k_prev, v_prev = kh_ref[...], vh_ref[...]
        else:
            k_prev = pc_ref[r0 - CHUNK:r0, gw:2 * gw]
            v_prev = pc_ref[r0 - CHUNK:r0, 2 * gw:3 * gw]
        k2 = jnp.concatenate([k_prev, pc_ref[r0:r0 + CHUNK, gw:2 * gw]], axis=0)
        v2 = jnp.concatenate([v_prev, pc_ref[r0:r0 + CHUNK, 2 * gw:3 * gw]], axis=0)
        qm = _mask_heads(pc_ref[r0:r0 + CHUNK, 0:gw], n_heads)
        logits = _dot_nt(qm, k2) + bias
        if c == 0:
            logits = jnp.where(jnp.logical_or(ki >= CHUNK, j > 0), logits, MASK_VALUE)
        mx = jnp.maximum(jnp.max(logits, axis=-1, keepdims=True), sink)
        e = jnp.exp(logits - mx)
        denom = jnp.sum(e, axis=-1, keepdims=True) + jnp.exp(sink - mx)
        probs = (e * (1.0 / denom)).astype(BF16)
        pcat = jnp.concatenate([probs[h * CHUNK:(h + 1) * CHUNK] for h in range(n_heads)], axis=1)
        yc = _dot(pcat, _mask_heads(v2, n_heads))
        o_ref[r0:r0 + CHUNK, 2 * gw:3 * gw] = _rms(yc, gain[:, 2 * gw:3 * gw]).astype(o_ref.dtype)


def _local_mix(pa, pc, rel_bias, sinks, sgu_w, sgu_b_tile, avg, pool_bd, pool_scale,
               bucket, gain, batch, seq):
    m = pa.shape[0]
    rows = LOCAL_ROWS
    nq = seq // rows
    gw = GROUP_WIDTH
    n_heads = gw // HEAD_DIM

    def row_blk(b, j):
        return b * nq + j

    halo_per_blk = rows // POOL_HALO
    kv_per_blk = rows // CHUNK
    smem = pl.BlockSpec(memory_space=pltpu.SMEM)
    whole = pl.BlockSpec(memory_space=pltpu.VMEM)
    return pl.pallas_call(
        _local_mix_kernel,
        grid=(batch, nq),
        in_specs=[
            smem, smem,
            pl.BlockSpec((rows, 3 * gw), lambda b, j: (row_blk(b, j), 0)),
            pl.BlockSpec((POOL_HALO, gw),
                         lambda b, j: (jnp.maximum(row_blk(b, j) * halo_per_blk - 1, 0), 2)),
            pl.BlockSpec((rows, 3 * gw), lambda b, j: (row_blk(b, j), 0)),
            pl.BlockSpec((CHUNK, gw),
                         lambda b, j: (jnp.maximum(row_blk(b, j) * kv_per_blk - 1, 0), 1)),
            pl.BlockSpec((CHUNK, gw),
                         lambda b, j: (jnp.maximum(row_blk(b, j) * kv_per_blk - 1, 0), 2)),
            whole, whole, whole, whole, whole, whole, whole,
        ],
        out_specs=pl.BlockSpec((rows, 3 * gw), lambda b, j: (row_blk(b, j), 0)),
        out_shape=jax.ShapeDtypeStruct((m, 3 * gw), BF16),
        scratch_shapes=[pltpu.VMEM((n_heads * CHUNK, 2 * CHUNK), F32),
                        pltpu.VMEM((CHUNK, n_heads * CHUNK), BF16)],
        compiler_params=pltpu.CompilerParams(
            dimension_semantics=("arbitrary", "arbitrary"), vmem_limit_bytes=VMEM_LIMIT_MIX),
        name="local_mix",
    )(rel_bias, sinks, pa, pa, pc, pc, pc, sgu_w, sgu_b_tile, avg, pool_bd, pool_scale,
      bucket, gain)


def _softplus(z):
    neg_abs = lax.bitcast_convert_type(
        lax.bitcast_convert_type(z, jnp.uint32) | jnp.uint32(0x80000000), F32)
    return jnp.maximum(z, 0.0) + jnp.log(1.0 + jnp.exp(neg_abs))


def _sb_attn_kernel(q_ref, k_ref, v_ref, upper_ref, gain_ref, o_ref,
                    qm_ref, acc_ref, carry_ref, cmin_ref):
    m = pl.program_id(1)
    n_sub = SB_SUBBLOCKS
    blk = q_ref.shape[0] // n_sub
    gw = q_ref.shape[1]
    half = gw // 2
    n_heads = gw // HEAD_DIM
    rows = n_heads * blk
    lane = lax.broadcasted_iota(jnp.int32, (1, gw), 1)
    lane_h = lax.broadcasted_iota(jnp.int32, (1, half), 1)

    for s in range(n_sub):
        q = q_ref[s * blk:(s + 1) * blk, :]
        for h in range(n_heads):
            qm_ref[s, h * blk:(h + 1) * blk, :] = jnp.where(lane // HEAD_DIM == h, q,
                                                            jnp.zeros_like(q))
    upper = upper_ref[...]
    causal = (lax.broadcasted_iota(jnp.int32, (blk, blk), 1)
              < lax.broadcasted_iota(jnp.int32, (blk, blk), 0))

    def keep_causal(x):
        return jnp.where(causal, x.reshape(n_heads, blk, blk), 0.0).reshape(rows, blk)

    def tile(s, t, carry, diag):
        start = pl.multiple_of((n_sub * m + s - t) * blk, blk)
        z = _dot_nt(qm_ref[s], k_ref[pl.ds(start, blk), :])
        sp = _softplus(z)
        log_sig = z - sp
        if diag:
            sp = keep_causal(sp)
        tail = _dot(sp.astype(BF16), upper)
        if carry is not None:
            tail = tail + jnp.concatenate([carry, carry], axis=1)
        w = jnp.exp(log_sig - tail)
        if diag:
            w = keep_causal(w)
        w = w.astype(BF16)
        rowsum = jnp.broadcast_to(jnp.sum(sp, axis=-1, keepdims=True), (rows, half))
        pv = []
        for p in range(n_heads // 2):
            r = _dot(w[2 * p * blk:(2 * p + 2) * blk],
                     v_ref[pl.ds(start, blk), p * half:(p + 1) * half])
            pv.append(jnp.where(lane_h < HEAD_DIM, r[:blk], r[blk:]))
        return pv, rowsum if carry is None else carry + rowsum

    def first_tiles(first_step):
        for s in range(n_sub):
            pv, carry = tile(s, 0, None, True)
            if s > 0 or not first_step:
                pv_b, carry = tile(s, 1, carry, False)
                pv = [a + b for a, b in zip(pv, pv_b)]
            carry_ref[s] = carry
            for p in range(n_heads // 2):
                acc_ref[s, p] = pv[p]
            cmin_ref[s] = jnp.min(carry)

    pl.when(m == 0)(functools.partial(first_tiles, True))
    pl.when(m > 0)(functools.partial(first_tiles, False))

    for s in range(n_sub):
        def more(state, s=s):
            t, cmin = state
            return jnp.logical_and(t <= n_sub * m + s, cmin <= EXP_UNDERFLOW)

        def visit(state, s=s):
            t, _ = state
            pv, carry = tile(s, t, carry_ref[s], False)
            carry_ref[s] = carry
            for p in range(n_heads // 2):
                acc_ref[s, p] += pv[p]
            return t + 1, jnp.min(carry)

        lax.while_loop(more, visit, (2, cmin_ref[s]))

    for s in range(n_sub):
        y = jnp.concatenate([acc_ref[s, p] for p in range(n_heads // 2)], axis=-1)
        o_ref[s * blk:(s + 1) * blk, :] = _rms(y, gain_ref[...]).astype(o_ref.dtype)


def _sb_attn(pd, upper, gain, batch, seq):
    m = pd.shape[0]
    gw = GROUP_WIDTH
    blk = SB_BLOCK
    n_sub = SB_SUBBLOCKS
    nq = seq // (n_sub * blk)
    n_heads = gw // HEAD_DIM
    whole = pl.BlockSpec(memory_space=pltpu.VMEM)
    return pl.pallas_call(
        _sb_attn_kernel,
        grid=(batch, nq),
        in_specs=[
            pl.BlockSpec((n_sub * blk, gw), lambda b, j: (b * nq + j, 0)),
            pl.BlockSpec((seq, gw), lambda b, j: (b, 1)),
            pl.BlockSpec((seq, gw), lambda b, j: (b, 2)),
            whole, whole,
        ],
        out_specs=pl.BlockSpec((n_sub * blk, gw), lambda b, j: (b * nq + j, 0)),
        out_shape=jax.ShapeDtypeStruct((m, gw), BF16),
        scratch_shapes=[pltpu.VMEM((n_sub, n_heads * blk, gw), BF16),
                        pltpu.VMEM((n_sub, n_heads // 2, blk, gw // 2), F32),
                        pltpu.VMEM((n_sub, n_heads * blk, gw // 2), F32),
                        pltpu.SMEM((n_sub,), F32)],
        compiler_params=pltpu.CompilerParams(
            dimension_semantics=("arbitrary", "arbitrary"), vmem_limit_bytes=VMEM_LIMIT_MIX),
        name="sb_attn",
    )(pd, pd, pd, upper, gain)


def _out_ffn_kernel(x_ref, yabc_ref, yd_ref, wout_ref, g_ref, wgu_ref, wdown_ref, gfin_ref,
                    o_ref, x2_ref, act_ref, *, final_norm):
    n_abc = yabc_ref.shape[1]
    d_ff = wdown_ref.shape[0]
    x2_ref[...] = (x_ref[...]
                   + _dot(yabc_ref[...], wout_ref[0:n_abc, :])
                   + _dot(yd_ref[...], wout_ref[n_abc:, :]))
    h = _rms(x2_ref[...], g_ref[...]).astype(BF16)
    for c in range(d_ff // FF_CHUNK):
        c0 = c * FF_CHUNK
        gate = _dot(h, wgu_ref[:, c0:c0 + FF_CHUNK])
        up = _dot(h, wgu_ref[:, d_ff + c0:d_ff + c0 + FF_CHUNK])
        act_ref[:, c0:c0 + FF_CHUNK] = (gate * (1.0 / (1.0 + jnp.exp(-gate))) * up).astype(BF16)
    out = x2_ref[...] + _dot(act_ref[...], wdown_ref[...])
    if final_norm:
        out = _rms(out, gfin_ref[...])
    o_ref[...] = out


def _out_ffn(x2d, yabc, yd, w_out, g, w_gu, w_down, g_final, layer, final_norm):
    m, d = x2d.shape
    tm = FFN_ROWS
    vec = pl.BlockSpec((1, d), lambda i: (0, 0))
    return pl.pallas_call(
        functools.partial(_out_ffn_kernel, final_norm=final_norm),
        grid=(m // tm,),
        in_specs=[
            pl.BlockSpec((tm, d), lambda i: (i, 0)),
            pl.BlockSpec((tm, yabc.shape[1]), lambda i: (i, 0)),
            pl.BlockSpec((tm, yd.shape[1]), lambda i: (i, 0)),
            _layer_weight(w_out, layer), vec,
            _layer_weight(w_gu, layer), _layer_weight(w_down, layer), vec,
        ],
        out_specs=pl.BlockSpec((tm, d), lambda i: (i, 0)),
        out_shape=jax.ShapeDtypeStruct((m, d), F32),
        scratch_shapes=[pltpu.VMEM((tm, d), F32), pltpu.VMEM((tm, w_down.shape[1]), BF16)],
        compiler_params=pltpu.CompilerParams(
            dimension_semantics=("arbitrary",), vmem_limit_bytes=VMEM_LIMIT_DENSE),
        name="out_ffn",
    )(x2d, yabc, yd, w_out, g, w_gu, w_down, g_final)


def _t5_bucket(dist):
    max_exact = N_BUCKETS // 2
    df = jnp.maximum(dist, 1).astype(F32)
    large = max_exact + (jnp.log(df / max_exact) / math.log(MAX_DISTANCE / max_exact)
                         * (N_BUCKETS - max_exact)).astype(jnp.int32)
    large = jnp.minimum(large, N_BUCKETS - 1)
    return jnp.where(dist < max_exact, dist, large)


def kernel(x, w_in, w_out, sgu_w, sgu_b, pool_w, pool_scale, swa_sinks, rel_bias, mix_out_gain,
           norm_mix, norm_ffn, w_gate_up, w_down, norm_final):
    batch, seq, d = x.shape
    depth = w_in.shape[0]
    gw = GROUP_WIDTH
    n_pool = len(POOL_WINDOWS)
    rows = batch * seq
    assert seq % LOCAL_ROWS == 0 and seq % (SB_BLOCK * SB_SUBBLOCKS) == 0
    assert rows % PROJ_ROWS == 0 and rows % FFN_ROWS == 0

    head_id = np.arange(gw) // HEAD_DIM
    avg = (head_id[:, None] == head_id[None, :]) / HEAD_DIM
    avg2 = jnp.asarray(np.concatenate([avg, avg], axis=0), BF16)
    upper = jnp.asarray(np.arange(SB_BLOCK)[:, None] > np.arange(SB_BLOCK)[None, :], BF16)
    dist = (jnp.arange(CHUNK)[:, None] + CHUNK) - jnp.arange(2 * CHUNK)[None, :]
    bucket = _t5_bucket(jnp.clip(dist, 0, CHUNK - 1)).astype(jnp.int32)

    w_in_b, w_out_b = w_in.astype(BF16), w_out.astype(BF16)
    w_gu_b, w_down_b = w_gate_up.astype(BF16), w_down.astype(BF16)
    pool_bd = (pool_w[:, :, :, None, :] * jnp.eye(n_pool, dtype=F32)[None, :, None, :, None])
    pool_bd = pool_bd.reshape(depth, gw, gw).astype(BF16)
    sgu_b_tile = jnp.repeat(jnp.swapaxes(sgu_b, 1, 2), HEAD_DIM, axis=2)
    gains = mix_out_gain.reshape(depth, 1, 4 * gw)

    x2d = x.reshape(rows, d)
    for l in range(depth):
        pa, pc, pd = _norm_proj(x2d, norm_mix[l].reshape(1, d), w_in_b, l)
        yabc = _local_mix(pa, pc, rel_bias, swa_sinks[l], sgu_w[l], sgu_b_tile[l], avg2,
                          pool_bd[l], pool_scale[l].reshape(1, gw), bucket,
                          gains[l, :, :3 * gw], batch, seq)
        yd = _sb_attn(pd, upper, gains[l, :, 3 * gw:], batch, seq)
        x2d = _out_ffn(x2d, yabc, yd, w_out_b, norm_ffn[l].reshape(1, d), w_gu_b, w_down_b,
                       norm_final.reshape(1, d), l, l == depth - 1)
    return x2d.reshape(batch, seq, d)
```

```python
import functools
import math

import numpy as np
import jax
import jax.numpy as jnp
from jax import lax
from jax.experimental import pallas as pl
from jax.experimental.pallas import tpu as pltpu

F32 = jnp.float32
BF16 = jnp.bfloat16

HEAD_DIM = 64
GROUP_WIDTH = 256
CHUNK = 128
POOL_WINDOWS = (2, 4, 8, 16)
POOL_HALO = 16
N_BUCKETS = 32
MAX_DISTANCE = 128
EPS = 1e-6
MASK_VALUE = -1e30
EXP_UNDERFLOW = 104.0

LOCAL_ROWS = 1024
SB_BLOCK = 256
SB_SUBBLOCKS = 8
PROJ_ROWS = 1024
FFN_ROWS = 1024
FF_CHUNK = 256
VMEM_LIMIT_DENSE = 56 * 1024 * 1024
VMEM_LIMIT_MIX = 40 * 1024 * 1024


def _rms(x, g):
    return x * lax.rsqrt(jnp.mean(x * x, axis=-1, keepdims=True) + EPS) * g


def _gelu_tanh(x):
    inner = math.sqrt(2.0 / math.pi) * (x + 0.044715 * (x * x * x))
    return x * (0.5 * (1.0 + jnp.tanh(inner)))


def _split_bf16(x):
    hi = x.astype(BF16)
    lo = (x - hi.astype(F32)).astype(BF16)
    return hi, lo


def _dot(a, b):
    return jnp.dot(a, b, preferred_element_type=F32)


def _dot_nt(a, b):
    return lax.dot_general(a, b, (((1,), (1,)), ((), ())), preferred_element_type=F32)


def _layer_weight(w, layer):
    return pl.BlockSpec((None,) + w.shape[1:], lambda *_: (layer, 0, 0),
                        pipeline_mode=pl.Buffered(1))


def _per_query_head(kv):
    lane = lax.broadcasted_iota(jnp.int32, (1, kv.shape[1]), 1)
    swapped = pltpu.roll(kv, HEAD_DIM, 1)
    first = jnp.where(lane < HEAD_DIM, kv, swapped)
    second = jnp.where(lane < HEAD_DIM, swapped, kv)
    return jnp.concatenate([first, second], axis=1)


def _norm_proj_kernel(x_ref, g_ref, w_ref, pa_ref, pc_ref, pd_ref):
    gw = GROUP_WIDTH
    kvw = w_ref.shape[1] - 7 * gw
    scale = HEAD_DIM ** -0.5
    h = _rms(x_ref[...], g_ref[...]).astype(BF16)
    y = _dot(h, w_ref[...])
    pa_ref[...] = y[:, :3 * gw]
    c_k = 4 * gw
    c_v = c_k + kvw // 2
    d_q = c_v + kvw // 2
    pc_ref[:, 0:gw] = (y[:, 3 * gw:4 * gw] * scale).astype(BF16)
    pc_ref[:, gw:2 * gw] = _per_query_head(y[:, c_k:c_v]).astype(BF16)
    pc_ref[:, 2 * gw:3 * gw] = _per_query_head(y[:, c_v:d_q]).astype(BF16)
    pd_ref[:, 0:gw] = (y[:, d_q:d_q + gw] * scale).astype(BF16)
    pd_ref[:, gw:3 * gw] = y[:, d_q + gw:].astype(BF16)


def _norm_proj(x2d, g, w, layer):
    m, d = x2d.shape
    tm = PROJ_ROWS
    na = nc = nd = 3 * GROUP_WIDTH
    assert w.shape[2] == 7 * GROUP_WIDTH + 4 * HEAD_DIM, "expects 2 kv heads of HEAD_DIM"
    return pl.pallas_call(
        _norm_proj_kernel,
        grid=(m // tm,),
        in_specs=[
            pl.BlockSpec((tm, d), lambda i: (i, 0)),
            pl.BlockSpec((1, d), lambda i: (0, 0)),
            _layer_weight(w, layer),
        ],
        out_specs=[
            pl.BlockSpec((tm, na), lambda i: (i, 0)),
            pl.BlockSpec((tm, nc), lambda i: (i, 0)),
            pl.BlockSpec((tm, nd), lambda i: (i, 0)),
        ],
        out_shape=[
            jax.ShapeDtypeStruct((m, na), F32),
            jax.ShapeDtypeStruct((m, nc), BF16),
            jax.ShapeDtypeStruct((m, nd), BF16),
        ],
        compiler_params=pltpu.CompilerParams(
            dimension_semantics=("arbitrary",), vmem_limit_bytes=VMEM_LIMIT_MIX),
        name="norm_proj",
    )(x2d, g, w)


def _mask_heads(x, n_heads):
    lane = lax.broadcasted_iota(jnp.int32, (1, x.shape[1]), 1)
    return jnp.concatenate(
        [jnp.where(lane // HEAD_DIM == h, x, jnp.zeros_like(x)) for h in range(n_heads)], axis=0)


def _local_mix_kernel(relb_ref, sink_ref,
                      pa_ref, halo_ref, pc_ref, kh_ref, vh_ref,
                      sguw_ref, sgub_ref, avg_ref, poolw_ref, pscale_ref,
                      bucket_ref, gain_ref,
                      o_ref,
                      bias_ref, wcat_ref):
    b = pl.program_id(0)
    j = pl.program_id(1)
    rows = pa_ref.shape[0]
    gw = GROUP_WIDTH
    n_heads = gw // HEAD_DIM
    n_chunks = rows // CHUNK
    lane = lax.broadcasted_iota(jnp.int32, (1, gw), 1)

    @pl.when(jnp.logical_and(b == 0, j == 0))
    def _():
        bucket = bucket_ref[...]
        tril = (lax.broadcasted_iota(jnp.int32, (CHUNK, CHUNK), 1)
                <= lax.broadcasted_iota(jnp.int32, (CHUNK, CHUNK), 0))
        qi = lax.broadcasted_iota(jnp.int32, bucket.shape, 0)
        ki = lax.broadcasted_iota(jnp.int32, bucket.shape, 1)
        in_window = jnp.logical_and(ki > qi, ki <= qi + CHUNK)
        for h in range(n_heads):
            acc = jnp.zeros(bucket.shape, F32)
            for bk in range(N_BUCKETS):
                acc = jnp.where(bucket == bk, relb_ref[bk, h], acc)
            bias_ref[h * CHUNK:(h + 1) * CHUNK, :] = jnp.where(in_window, acc, MASK_VALUE)
            wcat_ref[:, h * CHUNK:(h + 1) * CHUNK] = jnp.where(tril, sguw_ref[h], 0.0).astype(BF16)

    gain = gain_ref[...]

    p = pa_ref[:, 2 * gw:3 * gw]
    halo = jnp.where(j == 0, 0.0, halo_ref[...])
    ext = jnp.concatenate([halo, p], axis=0)
    sums = []
    run = ext
    for shift in (1, 2, 4, 8):
        run = run + pltpu.roll(run, shift, 0)
        sums.append(run[POOL_HALO:])
    group = lane // (gw // len(POOL_WINDOWS))
    wsum = jnp.where(group == 0, sums[0],
                     jnp.where(group == 1, sums[1],
                               jnp.where(group == 2, sums[2], sums[3])))
    win = jnp.where(group == 0, POOL_WINDOWS[0],
                    jnp.where(group == 1, POOL_WINDOWS[1],
                              jnp.where(group == 2, POOL_WINDOWS[2], POOL_WINDOWS[3])))
    t_seq = j * rows + lax.broadcasted_iota(jnp.int32, (rows, 1), 0)
    count = jnp.minimum(t_seq + 1, win).astype(F32)
    yb = wsum / count - p
    yb = _dot(yb.astype(BF16), poolw_ref[...]) * pscale_ref[...]
    o_ref[:, gw:2 * gw] = _rms(yb, gain[:, gw:2 * gw]).astype(o_ref.dtype)

    avg = avg_ref[...]
    u = _gelu_tanh(pa_ref[:, 0:gw])
    v = _gelu_tanh(pa_ref[:, gw:2 * gw])
    v_hi, v_lo = _split_bf16(v)
    vc = v - _dot(jnp.concatenate([v_hi, v_lo], axis=1), avg)
    sq_hi, sq_lo = _split_bf16(vc * vc)
    var = _dot(jnp.concatenate([sq_hi, sq_lo], axis=1), avg)
    vn = (vc * lax.rsqrt(var + EPS)).astype(BF16)
    rhs = jnp.concatenate(
        [_mask_heads(vn[c * CHUNK:(c + 1) * CHUNK], n_heads) for c in range(n_chunks)], axis=1)
    mix = _dot(wcat_ref[...], rhs)
    for c in range(n_chunks):
        r0 = c * CHUNK
        ya = u[r0:r0 + CHUNK] * (mix[:, c * gw:(c + 1) * gw] + sgub_ref[...])
        o_ref[r0:r0 + CHUNK, 0:gw] = _rms(ya, gain[:, 0:gw]).astype(o_ref.dtype)

    ki = lax.broadcasted_iota(jnp.int32, (1, 2 * CHUNK), 1)
    head_of_row = lax.broadcasted_iota(jnp.int32, (n_heads * CHUNK, 1), 0) // CHUNK
    sink = jnp.zeros((n_heads * CHUNK, 1), F32)
    for h in range(n_heads):
        sink = jnp.where(head_of_row == h, sink_ref[h], sink)
    bias = bias_ref[...]
    for c in range(n_chunks):
        r0 = c * CHUNK
        if c == 0:
            k_prev, v_prev = kh_ref[...], vh_ref[...]
        else:
            k_prev = pc_ref[r0 - CHUNK:r0, gw:2 * gw]
            v_prev = pc_ref[r0 - CHUNK:r0, 2 * gw:3 * gw]
        k2 = jnp.concatenate([k_prev, pc_ref[r0:r0 + CHUNK, gw:2 * gw]], axis=0)
        v2 = jnp.concatenate([v_prev, pc_ref[r0:r0 + CHUNK, 2 * gw:3 * gw]], axis=0)
        qm = _mask_heads(pc_ref[r0:r0 + CHUNK, 0:gw], n_heads)
        logits = _dot_nt(qm, k2) + bias
        if c == 0:
            logits = jnp.where(jnp.logical_or(ki >= CHUNK, j > 0), logits, MASK_VALUE)
        mx = jnp.maximum(jnp.max(logits, axis=-1, keepdims=True), sink)
        e = jnp.exp(logits - mx)
        denom = jnp.sum(e, axis=-1, keepdims=True) + jnp.exp(sink - mx)
        probs = (e * (1.0 / denom)).astype(BF16)
        pcat = jnp.concatenate([probs[h * CHUNK:(h + 1) * CHUNK] for h in range(n_heads)], axis=1)
        yc = _dot(pcat, _mask_heads(v2, n_heads))
        o_ref[r0:r0 + CHUNK, 2 * gw:3 * gw] = _rms(yc, gain[:, 2 * gw:3 * gw]).astype(o_ref.dtype)


def _local_mix(pa, pc, rel_bias, sinks, sgu_w, sgu_b_tile, avg, pool_bd, pool_scale,
               bucket, gain, batch, seq):
    m = pa.shape[0]
    rows = LOCAL_ROWS
    nq = seq // rows
    gw = GROUP_WIDTH
    n_heads = gw // HEAD_DIM

    def row_blk(b, j):
        return b * nq + j

    halo_per_blk = rows // POOL_HALO
    kv_per_blk = rows // CHUNK
    smem = pl.BlockSpec(memory_space=pltpu.SMEM)
    whole = pl.BlockSpec(memory_space=pltpu.VMEM)
    return pl.pallas_call(
        _local_mix_kernel,
        grid=(batch, nq),
        in_specs=[
            smem, smem,
            pl.BlockSpec((rows, 3 * gw), lambda b, j: (row_blk(b, j), 0)),
            pl.BlockSpec((POOL_HALO, gw),
                         lambda b, j: (jnp.maximum(row_blk(b, j) * halo_per_blk - 1, 0), 2)),
            pl.BlockSpec((rows, 3 * gw), lambda b, j: (row_blk(b, j), 0)),
            pl.BlockSpec((CHUNK, gw),
                         lambda b, j: (jnp.maximum(row_blk(b, j) * kv_per_blk - 1, 0), 1)),
            pl.BlockSpec((CHUNK, gw),
                         lambda b, j: (jnp.maximum(row_blk(b, j) * kv_per_blk - 1, 0), 2)),
            whole, whole, whole, whole, whole, whole, whole,
        ],
        out_specs=pl.BlockSpec((rows, 3 * gw), lambda b, j: (row_blk(b, j), 0)),
        out_shape=jax.ShapeDtypeStruct((m, 3 * gw), BF16),
        scratch_shapes=[pltpu.VMEM((n_heads * CHUNK, 2 * CHUNK), F32),
                        pltpu.VMEM((CHUNK, n_heads * CHUNK), BF16)],
        compiler_params=pltpu.CompilerParams(
            dimension_semantics=("arbitrary", "arbitrary"), vmem_limit_bytes=VMEM_LIMIT_MIX),
        name="local_mix",
    )(rel_bias, sinks, pa, pa, pc, pc, pc, sgu_w, sgu_b_tile, avg, pool_bd, pool_scale,
      bucket, gain)


def _softplus(z):
    neg_abs = lax.bitcast_convert_type(
        lax.bitcast_convert_type(z, jnp.uint32) | jnp.uint32(0x80000000), F32)
    return jnp.maximum(z, 0.0) + jnp.log(1.0 + jnp.exp(neg_abs))


def _sb_attn_kernel(q_ref, k_ref, v_ref, upper_ref, gain_ref, o_ref,
                    qm_ref, acc_ref, carry_ref, cmin_ref):
    m = pl.program_id(1)
    n_sub = SB_SUBBLOCKS
    blk = q_ref.shape[0] // n_sub
    gw = q_ref.shape[1]
    half = gw // 2
    n_heads = gw // HEAD_DIM
    rows = n_heads * blk
    lane = lax.broadcasted_iota(jnp.int32, (1, gw), 1)
    lane_h = lax.broadcasted_iota(jnp.int32, (1, half), 1)

    for s in range(n_sub):
        q = q_ref[s * blk:(s + 1) * blk, :]
        for h in range(n_heads):
            qm_ref[s, h * blk:(h + 1) * blk, :] = jnp.where(lane // HEAD_DIM == h, q,
                                                            jnp.zeros_like(q))
    upper = upper_ref[...]
    causal = (lax.broadcasted_iota(jnp.int32, (blk, blk), 1)
              < lax.broadcasted_iota(jnp.int32, (blk, blk), 0))

    def keep_causal(x):
        return jnp.where(causal, x.reshape(n_heads, blk, blk), 0.0).reshape(rows, blk)

    def tile(s, t, carry, diag):
        start = pl.multiple_of((n_sub * m + s - t) * blk, blk)
        z = _dot_nt(qm_ref[s], k_ref[pl.ds(start, blk), :])
        sp = _softplus(z)
        log_sig = z - sp
        if diag:
            sp = keep_causal(sp)
        tail = _dot(sp.astype(BF16), upper)
        if carry is not None:
            tail = tail + jnp.concatenate([carry, carry], axis=1)
        w = jnp.exp(log_sig - tail)
        if diag:
            w = keep_causal(w)
        w = w.astype(BF16)
        rowsum = jnp.broadcast_to(jnp.sum(sp, axis=-1, keepdims=True), (rows, half))
        pv = []
        for p in range(n_heads // 2):
            r = _dot(w[2 * p * blk:(2 * p + 2) * blk],
                     v_ref[pl.ds(start, blk), p * half:(p + 1) * half])
            pv.append(jnp.where(lane_h < HEAD_DIM, r[:blk], r[blk:]))
        return pv, rowsum if carry is None else carry + rowsum

    def first_tiles(first_step):
        for s in range(n_sub):
            pv, carry = tile(s, 0, None, True)
            if s > 0 or not first_step:
                pv_b, carry = tile(s, 1, carry, False)
                pv = [a + b for a, b in zip(pv, pv_b)]
            carry_ref[s] = carry
            for p in range(n_heads // 2):
                acc_ref[s, p] = pv[p]
            cmin_ref[s] = jnp.min(carry)

    pl.when(m == 0)(functools.partial(first_tiles, True))
    pl.when(m > 0)(functools.partial(first_tiles, False))

    for s in range(n_sub):
        def more(state, s=s):
            t, cmin = state
            return jnp.logical_and(t <= n_sub * m + s, cmin <= EXP_UNDERFLOW)

        def visit(state, s=s):
            t, _ = state
            pv, carry = tile(s, t, carry_ref[s], False)
            carry_ref[s] = carry
            for p in range(n_heads // 2):
                acc_ref[s, p] += pv[p]
            return t + 1, jnp.min(carry)

        lax.while_loop(more, visit, (2, cmin_ref[s]))

    for s in range(n_sub):
        y = jnp.concatenate([acc_ref[s, p] for p in range(n_heads // 2)], axis=-1)
        o_ref[s * blk:(s + 1) * blk, :] = _rms(y, gain_ref[...]).astype(o_ref.dtype)


def _sb_attn(pd, upper, gain, batch, seq):
    m = pd.shape[0]
    gw = GROUP_WIDTH
    blk = SB_BLOCK
    n_sub = SB_SUBBLOCKS
    nq = seq // (n_sub * blk)
    n_heads = gw // HEAD_DIM
    whole = pl.BlockSpec(memory_space=pltpu.VMEM)
    return pl.pallas_call(
        _sb_attn_kernel,
        grid=(batch, nq),
        in_specs=[
            pl.BlockSpec((n_sub * blk, gw), lambda b, j: (b * nq + j, 0)),
            pl.BlockSpec((seq, gw), lambda b, j: (b, 1)),
            pl.BlockSpec((seq, gw), lambda b, j: (b, 2)),
            whole, whole,
        ],
        out_specs=pl.BlockSpec((n_sub * blk, gw), lambda b, j: (b * nq + j, 0)),
        out_shape=jax.ShapeDtypeStruct((m, gw), BF16),
        scratch_shapes=[pltpu.VMEM((n_sub, n_heads * blk, gw), BF16),
                        pltpu.VMEM((n_sub, n_heads // 2, blk, gw // 2), F32),
                        pltpu.VMEM((n_sub, n_heads * blk, gw // 2), F32),
                        pltpu.SMEM((n_sub,), F32)],
        compiler_params=pltpu.CompilerParams(
            dimension_semantics=("arbitrary", "arbitrary"), vmem_limit_bytes=VMEM_LIMIT_MIX),
        name="sb_attn",
    )(pd, pd, pd, upper, gain)


def _out_ffn_kernel(x_ref, yabc_ref, yd_ref, wout_ref, g_ref, wgu_ref, wdown_ref, gfin_ref,
                    o_ref, x2_ref, act_ref, *, final_norm):
    n_abc = yabc_ref.shape[1]
    d_ff = wdown_ref.shape[0]
    x2_ref[...] = (x_ref[...]
                   + _dot(yabc_ref[...], wout_ref[0:n_abc, :])
                   + _dot(yd_ref[...], wout_ref[n_abc:, :]))
    h = _rms(x2_ref[...], g_ref[...]).astype(BF16)
    for c in range(d_ff // FF_CHUNK):
        c0 = c * FF_CHUNK
        gate = _dot(h, wgu_ref[:, c0:c0 + FF_CHUNK])
        up = _dot(h, wgu_ref[:, d_ff + c0:d_ff + c0 + FF_CHUNK])
        act_ref[:, c0:c0 + FF_CHUNK] = (gate * (1.0 / (1.0 + jnp.exp(-gate))) * up).astype(BF16)
    out = x2_ref[...] + _dot(act_ref[...], wdown_ref[...])
    if final_norm:
        out = _rms(out, gfin_ref[...])
    o_ref[...] = out


def _out_ffn(x2d, yabc, yd, w_out, g, w_gu, w_down, g_final, layer, final_norm):
    m, d = x2d.shape
    tm = FFN_ROWS
    vec = pl.BlockSpec((1, d), lambda i: (0, 0))
    return pl.pallas_call(
        functools.partial(_out_ffn_kernel, final_norm=final_norm),
        grid=(m // tm,),
        in_specs=[
            pl.BlockSpec((tm, d), lambda i: (i, 0)),
            pl.BlockSpec((tm, yabc.shape[1]), lambda i: (i, 0)),
            pl.BlockSpec((tm, yd.shape[1]), lambda i: (i, 0)),
            _layer_weight(w_out, layer), vec,
            _layer_weight(w_gu, layer), _layer_weight(w_down, layer), vec,
        ],
        out_specs=pl.BlockSpec((tm, d), lambda i: (i, 0)),
        out_shape=jax.ShapeDtypeStruct((m, d), F32),
        scratch_shapes=[pltpu.VMEM((tm, d), F32), pltpu.VMEM((tm, w_down.shape[1]), BF16)],
        compiler_params=pltpu.CompilerParams(
            dimension_semantics=("arbitrary",), vmem_limit_bytes=VMEM_LIMIT_DENSE),
        name="out_ffn",
    )(x2d, yabc, yd, w_out, g, w_gu, w_down, g_final)


def _t5_bucket(dist):
    max_exact = N_BUCKETS // 2
    df = jnp.maximum(dist, 1).astype(F32)
    large = max_exact + (jnp.log(df / max_exact) / math.log(MAX_DISTANCE / max_exact)
                         * (N_BUCKETS - max_exact)).astype(jnp.int32)
    large = jnp.minimum(large, N_BUCKETS - 1)
    return jnp.where(dist < max_exact, dist, large)


def kernel(x, w_in, w_out, sgu_w, sgu_b, pool_w, pool_scale, swa_sinks, rel_bias, mix_out_gain,
           norm_mix, norm_ffn, w_gate_up, w_down, norm_final):
    batch, seq, d = x.shape
    depth = w_in.shape[0]
    gw = GROUP_WIDTH
    n_pool = len(POOL_WINDOWS)
    rows = batch * seq
    assert seq % LOCAL_ROWS == 0 and seq % (SB_BLOCK * SB_SUBBLOCKS) == 0
    assert rows % PROJ_ROWS == 0 and rows % FFN_ROWS == 0

    head_id = np.arange(gw) // HEAD_DIM
    avg = (head_id[:, None] == head_id[None, :]) / HEAD_DIM
    avg2 = jnp.asarray(np.concatenate([avg, avg], axis=0), BF16)
    upper = jnp.asarray(np.arange(SB_BLOCK)[:, None] > np.arange(SB_BLOCK)[None, :], BF16)
    dist = (jnp.arange(CHUNK)[:, None] + CHUNK) - jnp.arange(2 * CHUNK)[None, :]
    bucket = _t5_bucket(jnp.clip(dist, 0, CHUNK - 1)).astype(jnp.int32)

    w_in_b, w_out_b = w_in.astype(BF16), w_out.astype(BF16)
    w_gu_b, w_down_b = w_gate_up.astype(BF16), w_down.astype(BF16)
    pool_bd = (pool_w[:, :, :, None, :] * jnp.eye(n_pool, dtype=F32)[None, :, None, :, None])
    pool_bd = pool_bd.reshape(depth, gw, gw).astype(BF16)
    sgu_b_tile = jnp.repeat(jnp.swapaxes(sgu_b, 1, 2), HEAD_DIM, axis=2)
    gains = mix_out_gain.reshape(depth, 1, 4 * gw)

    x2d = x.reshape(rows, d)
    for l in range(depth):
        pa, pc, pd = _norm_proj(x2d, norm_mix[l].reshape(1, d), w_in_b, l)
        yabc = _local_mix(pa, pc, rel_bias, swa_sinks[l], sgu_w[l], sgu_b_tile[l], avg2,
                          pool_bd[l], pool_scale[l].reshape(1, gw), bucket,
                          gains[l, :, :3 * gw], batch, seq)
        yd = _sb_attn(pd, upper, gains[l, :, 3 * gw:], batch, seq)
        x2d = _out_ffn(x2d, yabc, yd, w_out_b, norm_ffn[l].reshape(1, d), w_gu_b, w_down_b,
                       norm_final.reshape(1, d), l, l == depth - 1)
    return x2d.reshape(batch, seq, d)
```

```python
import functools
import math

import numpy as np
import jax
import jax.numpy as jnp
from jax import lax
from jax.experimental import pallas as pl
from jax.experimental.pallas import tpu as pltpu

F32 = jnp.float32
BF16 = jnp.bfloat16

HEAD_DIM = 64
GROUP_WIDTH = 256
CHUNK = 128
POOL_WINDOWS = (2, 4, 8, 16)
POOL_HALO = 16
N_BUCKETS = 32
MAX_DISTANCE = 128
EPS = 1e-6
MASK_VALUE = -1e30
EXP_UNDERFLOW = 104.0

SB_BLOCK = 256
SB_SUBBLOCKS = 8
PROJ_ROWS = 1024
FFN_ROWS = 1024
FF_CHUNK = 256
VMEM_LIMIT_DENSE = 56 * 1024 * 1024
VMEM_LIMIT_MIX = 40 * 1024 * 1024


def _rms(x, g):
    return x * lax.rsqrt(jnp.mean(x * x, axis=-1, keepdims=True) + EPS) * g


def _gelu_tanh(x):
    inner = math.sqrt(2.0 / math.pi) * (x + 0.044715 * (x * x * x))
    return x * (0.5 * (1.0 + jnp.tanh(inner)))


def _split_bf16(x):
    hi = x.astype(BF16)
    lo = (x - hi.astype(F32)).astype(BF16)
    return hi, lo


def _dot(a, b):
    return jnp.dot(a, b, preferred_element_type=F32)


def _dot_nt(a, b):
    return lax.dot_general(a, b, (((1,), (1,)), ((), ())), preferred_element_type=F32)


def _layer_weight(w, layer):
    return pl.BlockSpec((None,) + w.shape[1:], lambda *_: (layer, 0, 0),
                        pipeline_mode=pl.Buffered(1))


def _per_query_head(kv):
    lane = lax.broadcasted_iota(jnp.int32, (1, kv.shape[1]), 1)
    swapped = pltpu.roll(kv, HEAD_DIM, 1)
    first = jnp.where(lane < HEAD_DIM, kv, swapped)
    second = jnp.where(lane < HEAD_DIM, swapped, kv)
    return jnp.concatenate([first, second], axis=1)


def _project(h, w_ref, pa_ref, pc_ref, pd_ref, r0, side_work):
    gw = GROUP_WIDTH
    rows = h.shape[0]
    kvw = w_ref.shape[1] - 7 * gw
    scale = HEAD_DIM ** -0.5
    c_k = 4 * gw
    c_v = c_k + kvw // 2
    d_q = c_v + kvw // 2
    sl = slice(r0, r0 + rows)

    def cols(lo, hi):
        return _dot(h, w_ref[:, lo:hi])

    pa_ref[sl, 0:gw] = cols(0, gw)
    next(side_work, None)
    pa_ref[sl, gw:2 * gw] = cols(gw, 2 * gw)
    next(side_work, None)
    pa_ref[sl, 2 * gw:3 * gw] = cols(2 * gw, 3 * gw)
    next(side_work, None)
    pc_ref[sl, 0:gw] = (cols(3 * gw, 4 * gw) * scale).astype(BF16)
    next(side_work, None)
    kv = cols(c_k, d_q)
    pc_ref[sl, gw:2 * gw] = _per_query_head(kv[:, :kvw // 2]).astype(BF16)
    pc_ref[sl, 2 * gw:3 * gw] = _per_query_head(kv[:, kvw // 2:]).astype(BF16)
    next(side_work, None)
    pd_ref[sl, 0:gw] = (cols(d_q, d_q + gw) * scale).astype(BF16)
    next(side_work, None)
    pd_ref[sl, gw:3 * gw] = cols(d_q + gw, d_q + 3 * gw).astype(BF16)


def _proj_mix_kernel(relb_ref, sink_ref,
                     x_ref, g_ref, w_ref,
                     sguw_ref, sgub_ref, avg_ref, poolw_ref, pscale_ref, bucket_ref, gain_ref,
                     y_ref, pd_ref,
                     pa_ref, pc_ref, hb_ref, hk_ref, hv_ref, bias_ref, wcat_ref,
                     *, blocks_per_seq):
    i = pl.program_id(0)
    rows = x_ref.shape[0]
    half = rows // 2
    gw = GROUP_WIDTH
    seq_start = i % blocks_per_seq == 0
    t0 = (i % blocks_per_seq) * rows

    @pl.when(i == 0)
    def _():
        _local_mix_tables(relb_ref, sguw_ref, bucket_ref, bias_ref, wcat_ref)
        hb_ref[...] = jnp.zeros_like(hb_ref)
        hk_ref[...] = jnp.zeros_like(hk_ref)
        hv_ref[...] = jnp.zeros_like(hv_ref)

    h = _rms(x_ref[...], g_ref[...]).astype(BF16)

    def mixers(r0, first, halo_b, halo_k, halo_v):
        return _local_mix_pieces(
            first, t0 + r0, sink_ref,
            pa_ref.at[r0:r0 + half], halo_b, pc_ref.at[r0:r0 + half], halo_k, halo_v,
            sgub_ref, avg_ref, poolw_ref, pscale_ref, gain_ref,
            y_ref.at[r0:r0 + half], bias_ref, wcat_ref)

    _project(h[:half], w_ref, pa_ref, pc_ref, pd_ref, 0, iter(()))
    first_half = mixers(0, seq_start, hb_ref, hk_ref, hv_ref)
    _project(h[half:], w_ref, pa_ref, pc_ref, pd_ref, half, first_half)
    for _ in first_half:
        pass
    second_half = mixers(half, False,
                         pa_ref.at[half - POOL_HALO:half, 2 * gw:3 * gw],
                         pc_ref.at[half - CHUNK:half, gw:2 * gw],
                         pc_ref.at[half - CHUNK:half, 2 * gw:3 * gw])
    for _ in second_half:
        pass
    hb_ref[...] = pa_ref[rows - POOL_HALO:rows, 2 * gw:3 * gw]
    hk_ref[...] = pc_ref[rows - CHUNK:rows, gw:2 * gw]
    hv_ref[...] = pc_ref[rows - CHUNK:rows, 2 * gw:3 * gw]


def _proj_mix(x2d, g, w, layer, rel_bias, sinks, sgu_w, sgu_b_tile, avg, pool_bd, pool_scale,
              bucket, gain, seq):
    m, d = x2d.shape
    tm = PROJ_ROWS
    gw = GROUP_WIDTH
    n_heads = gw // HEAD_DIM
    assert w.shape[2] == 7 * gw + 4 * HEAD_DIM, "expects 2 kv heads of HEAD_DIM"
    assert seq % tm == 0 and (tm // 2) % CHUNK == 0
    smem = pl.BlockSpec(memory_space=pltpu.SMEM)
    whole = pl.BlockSpec(memory_space=pltpu.VMEM)
    return pl.pallas_call(
        functools.partial(_proj_mix_kernel, blocks_per_seq=seq // tm),
        grid=(m // tm,),
        in_specs=[
            smem, smem,
            pl.BlockSpec((tm, d), lambda i: (i, 0)),
            pl.BlockSpec((1, d), lambda i: (0, 0)),
            _layer_weight(w, layer),
            whole, whole, whole, whole, whole, whole, whole,
        ],
        out_specs=[
            pl.BlockSpec((tm, 3 * gw), lambda i: (i, 0)),
            pl.BlockSpec((tm, 3 * gw), lambda i: (i, 0)),
        ],
        out_shape=[
            jax.ShapeDtypeStruct((m, 3 * gw), BF16),
            jax.ShapeDtypeStruct((m, 3 * gw), BF16),
        ],
        scratch_shapes=[pltpu.VMEM((tm, 3 * gw), F32),
                        pltpu.VMEM((tm, 3 * gw), BF16),
                        pltpu.VMEM((POOL_HALO, gw), F32),
                        pltpu.VMEM((CHUNK, gw), BF16),
                        pltpu.VMEM((CHUNK, gw), BF16),
                        pltpu.VMEM((n_heads * CHUNK, 2 * CHUNK), F32),
                        pltpu.VMEM((CHUNK, n_heads * CHUNK), BF16)],
        compiler_params=pltpu.CompilerParams(
            dimension_semantics=("arbitrary",), vmem_limit_bytes=VMEM_LIMIT_MIX),
        name="proj_mix",
    )(rel_bias, sinks, x2d, g, w, sgu_w, sgu_b_tile, avg, pool_bd, pool_scale, bucket, gain)


def _mask_heads(x, n_heads):
    lane = lax.broadcasted_iota(jnp.int32, (1, x.shape[1]), 1)
    return jnp.concatenate(
        [jnp.where(lane // HEAD_DIM == h, x, jnp.zeros_like(x)) for h in range(n_heads)], axis=0)


def _local_mix_tables(relb_ref, sguw_ref, bucket_ref, bias_ref, wcat_ref):
    n_heads = sguw_ref.shape[0]
    bucket = bucket_ref[...]
    tril = (lax.broadcasted_iota(jnp.int32, (CHUNK, CHUNK), 1)
            <= lax.broadcasted_iota(jnp.int32, (CHUNK, CHUNK), 0))
    qi = lax.broadcasted_iota(jnp.int32, bucket.shape, 0)
    ki = lax.broadcasted_iota(jnp.int32, bucket.shape, 1)
    in_window = jnp.logical_and(ki > qi, ki <= qi + CHUNK)
    for h in range(n_heads):
        acc = jnp.zeros(bucket.shape, F32)
        for bk in range(N_BUCKETS):
            acc = jnp.where(bucket == bk, relb_ref[bk, h], acc)
        bias_ref[h * CHUNK:(h + 1) * CHUNK, :] = jnp.where(in_window, acc, MASK_VALUE)
        wcat_ref[:, h * CHUNK:(h + 1) * CHUNK] = jnp.where(tril, sguw_ref[h], 0.0).astype(BF16)


def _local_mix_pieces(seq_start, t0, sink_ref,
                      pa_ref, halo_ref, pc_ref, kh_ref, vh_ref,
                      sgub_ref, avg_ref, poolw_ref, pscale_ref, gain_ref,
                      o_ref, bias_ref, wcat_ref):
    rows = pa_ref.shape[0]
    gw = GROUP_WIDTH
    n_heads = gw // HEAD_DIM
    n_chunks = rows // CHUNK
    lane = lax.broadcasted_iota(jnp.int32, (1, gw), 1)
    gain = gain_ref[...]

    p = pa_ref[:, 2 * gw:3 * gw]
    halo = jnp.where(seq_start, 0.0, halo_ref[...])
    ext = jnp.concatenate([halo, p], axis=0)
    sums = []
    run = ext
    for shift in (1, 2, 4, 8):
        run = run + pltpu.roll(run, shift, 0)
        sums.append(run[POOL_HALO:])
    group = lane // (gw // len(POOL_WINDOWS))
    wsum = jnp.where(group == 0, sums[0],
                     jnp.where(group == 1, sums[1],
                               jnp.where(group == 2, sums[2], sums[3])))
    win = jnp.where(group == 0, POOL_WINDOWS[0],
                    jnp.where(group == 1, POOL_WINDOWS[1],
                              jnp.where(group == 2, POOL_WINDOWS[2], POOL_WINDOWS[3])))
    t_seq = t0 + lax.broadcasted_iota(jnp.int32, (rows, 1), 0)
    count = jnp.minimum(t_seq + 1, win).astype(F32)
    yb = wsum / count - p
    yb = _dot(yb.astype(BF16), poolw_ref[...]) * pscale_ref[...]
    o_ref[:, gw:2 * gw] = _rms(yb, gain[:, gw:2 * gw]).astype(o_ref.dtype)
    yield

    avg = avg_ref[...]
    u = _gelu_tanh(pa_ref[:, 0:gw])
    v = _gelu_tanh(pa_ref[:, gw:2 * gw])
    v_hi, v_lo = _split_bf16(v)
    vc = v - _dot(jnp.concatenate([v_hi, v_lo], axis=1), avg)
    sq_hi, sq_lo = _split_bf16(vc * vc)
    var = _dot(jnp.concatenate([sq_hi, sq_lo], axis=1), avg)
    vn = (vc * lax.rsqrt(var + EPS)).astype(BF16)
    rhs = jnp.concatenate(
        [_mask_heads(vn[c * CHUNK:(c + 1) * CHUNK], n_heads) for c in range(n_chunks)], axis=1)
    mix = _dot(wcat_ref[...], rhs)
    for c in range(n_chunks):
        r0 = c * CHUNK
        ya = u[r0:r0 + CHUNK] * (mix[:, c * gw:(c + 1) * gw] + sgub_ref[...])
        o_ref[r0:r0 + CHUNK, 0:gw] = _rms(ya, gain[:, 0:gw]).astype(o_ref.dtype)
    yield

    ki = lax.broadcasted_iota(jnp.int32, (1, 2 * CHUNK), 1)
    head_of_row = lax.broadcasted_iota(jnp.int32, (n_heads * CHUNK, 1), 0) // CHUNK
    sink = jnp.zeros((n_heads * CHUNK, 1), F32)
    for h in range(n_heads):
        sink = jnp.where(head_of_row == h, sink_ref[h], sink)
    bias = bias_ref[...]
    for c in range(n_chunks):
        r0 = c * CHUNK
        if c == 0:
            k_prev, v_prev = kh_ref[...], vh_ref[...]
        else:
            k_prev = pc_ref[r0 - CHUNK:r0, gw:2 * gw]
            v_prev = pc_ref[r0 - CHUNK:r0, 2 * gw:3 * gw]
        k2 = jnp.concatenate([k_prev, pc_ref[r0:r0 + CHUNK, gw:2 * gw]], axis=0)
        v2 = jnp.concatenate([v_prev, pc_ref[r0:r0 + CHUNK, 2 * gw:3 * gw]], axis=0)
        qm = _mask_heads(pc_ref[r0:r0 + CHUNK, 0:gw], n_heads)
        logits = _dot_nt(qm, k2) + bias
        if c == 0:
            logits = jnp.where(jnp.logical_or(ki >= CHUNK, jnp.logical_not(seq_start)),
                               logits, MASK_VALUE)
        mx = jnp.maximum(jnp.max(logits, axis=-1, keepdims=True), sink)
        e = jnp.exp(logits - mx)
        denom = jnp.sum(e, axis=-1, keepdims=True) + jnp.exp(sink - mx)
        probs = (e * (1.0 / denom)).astype(BF16)
        pcat = jnp.concatenate([probs[h * CHUNK:(h + 1) * CHUNK] for h in range(n_heads)], axis=1)
        yc = _dot(pcat, _mask_heads(v2, n_heads))
        o_ref[r0:r0 + CHUNK, 2 * gw:3 * gw] = _rms(yc, gain[:, 2 * gw:3 * gw]).astype(o_ref.dtype)
        yield


def _softplus(z):
    neg_abs = lax.bitcast_convert_type(
        lax.bitcast_convert_type(z, jnp.uint32) | jnp.uint32(0x80000000), F32)
    return jnp.maximum(z, 0.0) + jnp.log(1.0 + jnp.exp(neg_abs))


def _sb_attn_kernel(q_ref, k_ref, v_ref, upper_ref, gain_ref, o_ref,
                    qm_ref, acc_ref, carry_ref, cmin_ref):
    m = pl.program_id(1)
    n_sub = SB_SUBBLOCKS
    blk = q_ref.shape[0] // n_sub
    gw = q_ref.shape[1]
    half = gw // 2
    n_heads = gw // HEAD_DIM
    rows = n_heads * blk
    lane = lax.broadcasted_iota(jnp.int32, (1, gw), 1)
    lane_h = lax.broadcasted_iota(jnp.int32, (1, half), 1)

    for s in range(n_sub):
        q = q_ref[s * blk:(s + 1) * blk, :]
        for h in range(n_heads):
            qm_ref[s, h * blk:(h + 1) * blk, :] = jnp.where(lane // HEAD_DIM == h, q,
                                                            jnp.zeros_like(q))
    upper = upper_ref[...]
    causal = (lax.broadcasted_iota(jnp.int32, (blk, blk), 1)
              < lax.broadcasted_iota(jnp.int32, (blk, blk), 0))

    def keep_causal(x):
        return jnp.where(causal, x.reshape(n_heads, blk, blk), 0.0).reshape(rows, blk)

    def tile(s, t, carry, diag):
        start = pl.multiple_of((n_sub * m + s - t) * blk, blk)
        z = _dot_nt(qm_ref[s], k_ref[pl.ds(start, blk), :])
        sp = _softplus(z)
        log_sig = z - sp
        if diag:
            sp = keep_causal(sp)
        tail = _dot(sp.astype(BF16), upper)
        if carry is not None:
            tail = tail + jnp.concatenate([carry, carry], axis=1)
        w = jnp.exp(log_sig - tail)
        if diag:
            w = keep_causal(w)
        w = w.astype(BF16)
        rowsum = jnp.broadcast_to(jnp.sum(sp, axis=-1, keepdims=True), (rows, half))
        pv = []
        for p in range(n_heads // 2):
            r = _dot(w[2 * p * blk:(2 * p + 2) * blk],
                     v_ref[pl.ds(start, blk), p * half:(p + 1) * half])
            pv.append(jnp.where(lane_h < HEAD_DIM, r[:blk], r[blk:]))
        return pv, rowsum if carry is None else carry + rowsum

    def first_tiles(first_step):
        for s in range(n_sub):
            pv, carry = tile(s, 0, None, True)
            if s > 0 or not first_step:
                pv_b, carry = tile(s, 1, carry, False)
                pv = [a + b for a, b in zip(pv, pv_b)]
            carry_ref[s] = carry
            for p in range(n_heads // 2):
                acc_ref[s, p] = pv[p]
            cmin_ref[s] = jnp.min(carry)

    pl.when(m == 0)(functools.partial(first_tiles, True))
    pl.when(m > 0)(functools.partial(first_tiles, False))

    for s in range(n_sub):
        def more(state, s=s):
            t, cmin = state
            return jnp.logical_and(t <= n_sub * m + s, cmin <= EXP_UNDERFLOW)

        def visit(state, s=s):
            t, _ = state
            pv, carry = tile(s, t, carry_ref[s], False)
            carry_ref[s] = carry
            for p in range(n_heads // 2):
                acc_ref[s, p] += pv[p]
            return t + 1, jnp.min(carry)

        lax.while_loop(more, visit, (2, cmin_ref[s]))

    for s in range(n_sub):
        y = jnp.concatenate([acc_ref[s, p] for p in range(n_heads // 2)], axis=-1)
        o_ref[s * blk:(s + 1) * blk, :] = _rms(y, gain_ref[...]).astype(o_ref.dtype)


def _sb_attn(pd, upper, gain, batch, seq):
    m = pd.shape[0]
    gw = GROUP_WIDTH
    blk = SB_BLOCK
    n_sub = SB_SUBBLOCKS
    nq = seq // (n_sub * blk)
    n_heads = gw // HEAD_DIM
    whole = pl.BlockSpec(memory_space=pltpu.VMEM)
    return pl.pallas_call(
        _sb_attn_kernel,
        grid=(batch, nq),
        in_specs=[
            pl.BlockSpec((n_sub * blk, gw), lambda b, j: (b * nq + j, 0)),
            pl.BlockSpec((seq, gw), lambda b, j: (b, 1)),
            pl.BlockSpec((seq, gw), lambda b, j: (b, 2)),
            whole, whole,
        ],
        out_specs=pl.BlockSpec((n_sub * blk, gw), lambda b, j: (b * nq + j, 0)),
        out_shape=jax.ShapeDtypeStruct((m, gw), BF16),
        scratch_shapes=[pltpu.VMEM((n_sub, n_heads * blk, gw), BF16),
                        pltpu.VMEM((n_sub, n_heads // 2, blk, gw // 2), F32),
                        pltpu.VMEM((n_sub, n_heads * blk, gw // 2), F32),
                        pltpu.SMEM((n_sub,), F32)],
        compiler_params=pltpu.CompilerParams(
            dimension_semantics=("arbitrary", "arbitrary"), vmem_limit_bytes=VMEM_LIMIT_MIX),
        name="sb_attn",
    )(pd, pd, pd, upper, gain)


def _out_ffn_kernel(x_ref, yabc_ref, yd_ref, wout_ref, g_ref, wgu_ref, wdown_ref, gfin_ref,
                    o_ref, x2_ref, act_ref, *, final_norm):
    n_abc = yabc_ref.shape[1]
    d_ff = wdown_ref.shape[0]
    x2_ref[...] = (x_ref[...]
                   + _dot(yabc_ref[...], wout_ref[0:n_abc, :])
                   + _dot(yd_ref[...], wout_ref[n_abc:, :]))
    h = _rms(x2_ref[...], g_ref[...]).astype(BF16)
    for c in range(d_ff // FF_CHUNK):
        c0 = c * FF_CHUNK
        gate = _dot(h, wgu_ref[:, c0:c0 + FF_CHUNK])
        up = _dot(h, wgu_ref[:, d_ff + c0:d_ff + c0 + FF_CHUNK])
        act_ref[:, c0:c0 + FF_CHUNK] = (gate * (1.0 / (1.0 + jnp.exp(-gate))) * up).astype(BF16)
    out = x2_ref[...] + _dot(act_ref[...], wdown_ref[...])
    if final_norm:
        out = _rms(out, gfin_ref[...])
    o_ref[...] = out


def _out_ffn(x2d, yabc, yd, w_out, g, w_gu, w_down, g_final, layer, final_norm):
    m, d = x2d.shape
    tm = FFN_ROWS
    vec = pl.BlockSpec((1, d), lambda i: (0, 0))
    return pl.pallas_call(
        functools.partial(_out_ffn_kernel, final_norm=final_norm),
        grid=(m // tm,),
        in_specs=[
            pl.BlockSpec((tm, d), lambda i: (i, 0)),
            pl.BlockSpec((tm, yabc.shape[1]), lambda i: (i, 0)),
            pl.BlockSpec((tm, yd.shape[1]), lambda i: (i, 0)),
            _layer_weight(w_out, layer), vec,
            _layer_weight(w_gu, layer), _layer_weight(w_down, layer), vec,
        ],
        out_specs=pl.BlockSpec((tm, d), lambda i: (i, 0)),
        out_shape=jax.ShapeDtypeStruct((m, d), F32),
        scratch_shapes=[pltpu.VMEM((tm, d), F32), pltpu.VMEM((tm, w_down.shape[1]), BF16)],
        compiler_params=pltpu.CompilerParams(
            dimension_semantics=("arbitrary",), vmem_limit_bytes=VMEM_LIMIT_DENSE),
        name="out_ffn",
    )(x2d, yabc, yd, w_out, g, w_gu, w_down, g_final)


def _t5_bucket(dist):
    max_exact = N_BUCKETS // 2
    df = jnp.maximum(dist, 1).astype(F32)
    large = max_exact + (jnp.log(df / max_exact) / math.log(MAX_DISTANCE / max_exact)
                         * (N_BUCKETS - max_exact)).astype(jnp.int32)
    large = jnp.minimum(large, N_BUCKETS - 1)
    return jnp.where(dist < max_exact, dist, large)


def kernel(x, w_in, w_out, sgu_w, sgu_b, pool_w, pool_scale, swa_sinks, rel_bias, mix_out_gain,
           norm_mix, norm_ffn, w_gate_up, w_down, norm_final):
    batch, seq, d = x.shape
    depth = w_in.shape[0]
    gw = GROUP_WIDTH
    n_pool = len(POOL_WINDOWS)
    rows = batch * seq
    assert seq % (SB_BLOCK * SB_SUBBLOCKS) == 0
    assert rows % PROJ_ROWS == 0 and rows % FFN_ROWS == 0

    head_id = np.arange(gw) // HEAD_DIM
    avg = (head_id[:, None] == head_id[None, :]) / HEAD_DIM
    avg2 = jnp.asarray(np.concatenate([avg, avg], axis=0), BF16)
    upper = jnp.asarray(np.arange(SB_BLOCK)[:, None] > np.arange(SB_BLOCK)[None, :], BF16)
    dist = (jnp.arange(CHUNK)[:, None] + CHUNK) - jnp.arange(2 * CHUNK)[None, :]
    bucket = _t5_bucket(jnp.clip(dist, 0, CHUNK - 1)).astype(jnp.int32)

    w_in_b, w_out_b = w_in.astype(BF16), w_out.astype(BF16)
    w_gu_b, w_down_b = w_gate_up.astype(BF16), w_down.astype(BF16)
    pool_bd = (pool_w[:, :, :, None, :] * jnp.eye(n_pool, dtype=F32)[None, :, None, :, None])
    pool_bd = pool_bd.reshape(depth, gw, gw).astype(BF16)
    sgu_b_tile = jnp.repeat(jnp.swapaxes(sgu_b, 1, 2), HEAD_DIM, axis=2)
    gains = mix_out_gain.reshape(depth, 1, 4 * gw)

    x2d = x.reshape(rows, d)
    for l in range(depth):
        yabc, pd = _proj_mix(x2d, norm_mix[l].reshape(1, d), w_in_b, l, rel_bias, swa_sinks[l],
                             sgu_w[l], sgu_b_tile[l], avg2, pool_bd[l], pool_scale[l].reshape(1, gw),
                             bucket, gains[l, :, :3 * gw], seq)
        yd = _sb_attn(pd, upper, gains[l, :, 3 * gw:], batch, seq)
        x2d = _out_ffn(x2d, yabc, yd, w_out_b, norm_ffn[l].reshape(1, d), w_gu_b, w_down_b,
                       norm_final.reshape(1, d), l, l == depth - 1)
    return x2d.reshape(batch, seq, d)
```

```python
import functools
import math

import numpy as np
import jax
import jax.numpy as jnp
from jax import lax
from jax.experimental import pallas as pl
from jax.experimental.pallas import tpu as pltpu

F32 = jnp.float32
BF16 = jnp.bfloat16

HEAD_DIM = 64
GROUP_WIDTH = 256
CHUNK = 128
POOL_WINDOWS = (2, 4, 8, 16)
POOL_HALO = 16
N_BUCKETS = 32
MAX_DISTANCE = 128
EPS = 1e-6
MASK_VALUE = -1e30
EXP_UNDERFLOW = 104.0

SB_BLOCK = 256
SB_SUBBLOCKS = 8
PROJ_ROWS = 2048
PROJ_PARTS = 2
FFN_ROWS = 1024
FF_CHUNK = 256
VMEM_LIMIT_DENSE = 56 * 1024 * 1024
VMEM_LIMIT_MIX = 40 * 1024 * 1024


def _rms(x, g):
    return x * lax.rsqrt(jnp.mean(x * x, axis=-1, keepdims=True) + EPS) * g


def _gelu_tanh(x):
    inner = math.sqrt(2.0 / math.pi) * (x + 0.044715 * (x * x * x))
    return x * (0.5 * (1.0 + jnp.tanh(inner)))


def _split_bf16(x):
    hi = x.astype(BF16)
    lo = (x - hi.astype(F32)).astype(BF16)
    return hi, lo


def _dot(a, b):
    return jnp.dot(a, b, preferred_element_type=F32)


def _dot_nt(a, b):
    return lax.dot_general(a, b, (((1,), (1,)), ((), ())), preferred_element_type=F32)


def _layer_weight(w, layer):
    return pl.BlockSpec((None,) + w.shape[1:], lambda *_: (layer, 0, 0),
                        pipeline_mode=pl.Buffered(1))


def _per_query_head(kv):
    lane = lax.broadcasted_iota(jnp.int32, (1, kv.shape[1]), 1)
    swapped = pltpu.roll(kv, HEAD_DIM, 1)
    first = jnp.where(lane < HEAD_DIM, kv, swapped)
    second = jnp.where(lane < HEAD_DIM, swapped, kv)
    return jnp.concatenate([first, second], axis=1)


def _project(h, w_ref, pa_ref, pc_ref, pd_ref, r0, side_work):
    gw = GROUP_WIDTH
    rows = h.shape[0]
    kvw = w_ref.shape[1] - 7 * gw
    scale = HEAD_DIM ** -0.5
    c_k = 4 * gw
    c_v = c_k + kvw // 2
    d_q = c_v + kvw // 2
    sl = slice(r0, r0 + rows)

    def cols(lo, hi):
        return _dot(h, w_ref[:, lo:hi])

    pa_ref[sl, 0:gw] = cols(0, gw)
    next(side_work, None)
    pa_ref[sl, gw:2 * gw] = cols(gw, 2 * gw)
    next(side_work, None)
    pa_ref[sl, 2 * gw:3 * gw] = cols(2 * gw, 3 * gw)
    next(side_work, None)
    pc_ref[sl, 0:gw] = (cols(3 * gw, 4 * gw) * scale).astype(BF16)
    next(side_work, None)
    kv = cols(c_k, d_q)
    pc_ref[sl, gw:2 * gw] = _per_query_head(kv[:, :kvw // 2]).astype(BF16)
    pc_ref[sl, 2 * gw:3 * gw] = _per_query_head(kv[:, kvw // 2:]).astype(BF16)
    next(side_work, None)
    pd_ref[sl, 0:gw] = (cols(d_q, d_q + gw) * scale).astype(BF16)
    next(side_work, None)
    pd_ref[sl, gw:3 * gw] = cols(d_q + gw, d_q + 3 * gw).astype(BF16)


def _proj_mix_kernel(relb_ref, sink_ref,
                     x_ref, g_ref, w_ref,
                     sguw_ref, sgub_ref, avg_ref, poolw_ref, pscale_ref, bucket_ref, gain_ref,
                     y_ref, pd_ref,
                     pa_ref, pc_ref, hb_ref, hk_ref, hv_ref, bias_ref, wcat_ref,
                     *, blocks_per_seq):
    i = pl.program_id(0)
    rows = x_ref.shape[0]
    part = rows // PROJ_PARTS
    gw = GROUP_WIDTH
    seq_start = i % blocks_per_seq == 0
    t0 = (i % blocks_per_seq) * rows

    @pl.when(i == 0)
    def _():
        _local_mix_tables(relb_ref, sguw_ref, bucket_ref, bias_ref, wcat_ref)
        hb_ref[...] = jnp.zeros_like(hb_ref)
        hk_ref[...] = jnp.zeros_like(hk_ref)
        hv_ref[...] = jnp.zeros_like(hv_ref)

    h = _rms(x_ref[...], g_ref[...]).astype(BF16)

    def mixers(p):
        r0 = p * part
        if p == 0:
            first, halos = seq_start, (hb_ref, hk_ref, hv_ref)
        else:
            first = False
            halos = (pa_ref.at[r0 - POOL_HALO:r0, 2 * gw:3 * gw],
                     pc_ref.at[r0 - CHUNK:r0, gw:2 * gw],
                     pc_ref.at[r0 - CHUNK:r0, 2 * gw:3 * gw])
        return _local_mix_pieces(
            first, t0 + r0, sink_ref,
            pa_ref.at[r0:r0 + part], halos[0], pc_ref.at[r0:r0 + part], halos[1], halos[2],
            sgub_ref, avg_ref, poolw_ref, pscale_ref, gain_ref,
            y_ref.at[r0:r0 + part], bias_ref, wcat_ref)

    side_work = iter(())
    for p in range(PROJ_PARTS):
        _project(h[p * part:(p + 1) * part], w_ref, pa_ref, pc_ref, pd_ref, p * part, side_work)
        for _ in side_work:
            pass
        side_work = mixers(p)
    for _ in side_work:
        pass
    hb_ref[...] = pa_ref[rows - POOL_HALO:rows, 2 * gw:3 * gw]
    hk_ref[...] = pc_ref[rows - CHUNK:rows, gw:2 * gw]
    hv_ref[...] = pc_ref[rows - CHUNK:rows, 2 * gw:3 * gw]


def _proj_mix(x2d, g, w, layer, rel_bias, sinks, sgu_w, sgu_b_tile, avg, pool_bd, pool_scale,
              bucket, gain, seq):
    m, d = x2d.shape
    tm = PROJ_ROWS
    gw = GROUP_WIDTH
    n_heads = gw // HEAD_DIM
    assert w.shape[2] == 7 * gw + 4 * HEAD_DIM, "expects 2 kv heads of HEAD_DIM"
    assert seq % tm == 0 and (tm // PROJ_PARTS) % CHUNK == 0
    smem = pl.BlockSpec(memory_space=pltpu.SMEM)
    whole = pl.BlockSpec(memory_space=pltpu.VMEM)
    return pl.pallas_call(
        functools.partial(_proj_mix_kernel, blocks_per_seq=seq // tm),
        grid=(m // tm,),
        in_specs=[
            smem, smem,
            pl.BlockSpec((tm, d), lambda i: (i, 0)),
            pl.BlockSpec((1, d), lambda i: (0, 0)),
            _layer_weight(w, layer),
            whole, whole, whole, whole, whole, whole, whole,
        ],
        out_specs=[
            pl.BlockSpec((tm, 3 * gw), lambda i: (i, 0)),
            pl.BlockSpec((tm, 3 * gw), lambda i: (i, 0)),
        ],
        out_shape=[
            jax.ShapeDtypeStruct((m, 3 * gw), BF16),
            jax.ShapeDtypeStruct((m, 3 * gw), BF16),
        ],
        scratch_shapes=[pltpu.VMEM((tm, 3 * gw), F32),
                        pltpu.VMEM((tm, 3 * gw), BF16),
                        pltpu.VMEM((POOL_HALO, gw), F32),
                        pltpu.VMEM((CHUNK, gw), BF16),
                        pltpu.VMEM((CHUNK, gw), BF16),
                        pltpu.VMEM((n_heads * CHUNK, 2 * CHUNK), F32),
                        pltpu.VMEM((CHUNK, n_heads * CHUNK), BF16)],
        compiler_params=pltpu.CompilerParams(
            dimension_semantics=("arbitrary",), vmem_limit_bytes=VMEM_LIMIT_DENSE),
        name="proj_mix",
    )(rel_bias, sinks, x2d, g, w, sgu_w, sgu_b_tile, avg, pool_bd, pool_scale, bucket, gain)


def _mask_heads(x, n_heads):
    lane = lax.broadcasted_iota(jnp.int32, (1, x.shape[1]), 1)
    return jnp.concatenate(
        [jnp.where(lane // HEAD_DIM == h, x, jnp.zeros_like(x)) for h in range(n_heads)], axis=0)


def _local_mix_tables(relb_ref, sguw_ref, bucket_ref, bias_ref, wcat_ref):
    n_heads = sguw_ref.shape[0]
    bucket = bucket_ref[...]
    tril = (lax.broadcasted_iota(jnp.int32, (CHUNK, CHUNK), 1)
            <= lax.broadcasted_iota(jnp.int32, (CHUNK, CHUNK), 0))
    qi = lax.broadcasted_iota(jnp.int32, bucket.shape, 0)
    ki = lax.broadcasted_iota(jnp.int32, bucket.shape, 1)
    in_window = jnp.logical_and(ki > qi, ki <= qi + CHUNK)
    for h in range(n_heads):
        acc = jnp.zeros(bucket.shape, F32)
        for bk in range(N_BUCKETS):
            acc = jnp.where(bucket == bk, relb_ref[bk, h], acc)
        bias_ref[h * CHUNK:(h + 1) * CHUNK, :] = jnp.where(in_window, acc, MASK_VALUE)
        wcat_ref[:, h * CHUNK:(h + 1) * CHUNK] = jnp.where(tril, sguw_ref[h], 0.0).astype(BF16)


def _local_mix_pieces(seq_start, t0, sink_ref,
                      pa_ref, halo_ref, pc_ref, kh_ref, vh_ref,
                      sgub_ref, avg_ref, poolw_ref, pscale_ref, gain_ref,
                      o_ref, bias_ref, wcat_ref):
    rows = pa_ref.shape[0]
    gw = GROUP_WIDTH
    n_heads = gw // HEAD_DIM
    n_chunks = rows // CHUNK
    lane = lax.broadcasted_iota(jnp.int32, (1, gw), 1)
    gain = gain_ref[...]

    p = pa_ref[:, 2 * gw:3 * gw]
    halo = jnp.where(seq_start, 0.0, halo_ref[...])
    ext = jnp.concatenate([halo, p], axis=0)
    sums = []
    run = ext
    for shift in (1, 2, 4, 8):
        run = run + pltpu.roll(run, shift, 0)
        sums.append(run[POOL_HALO:])
    group = lane // (gw // len(POOL_WINDOWS))
    wsum = jnp.where(group == 0, sums[0],
                     jnp.where(group == 1, sums[1],
                               jnp.where(group == 2, sums[2], sums[3])))
    win = jnp.where(group == 0, POOL_WINDOWS[0],
                    jnp.where(group == 1, POOL_WINDOWS[1],
                              jnp.where(group == 2, POOL_WINDOWS[2], POOL_WINDOWS[3])))
    t_seq = t0 + lax.broadcasted_iota(jnp.int32, (rows, 1), 0)
    count = jnp.minimum(t_seq + 1, win).astype(F32)
    yb = wsum / count - p
    yb = _dot(yb.astype(BF16), poolw_ref[...]) * pscale_ref[...]
    o_ref[:, gw:2 * gw] = _rms(yb, gain[:, gw:2 * gw]).astype(o_ref.dtype)
    yield

    avg = avg_ref[...]
    u = _gelu_tanh(pa_ref[:, 0:gw])
    v = _gelu_tanh(pa_ref[:, gw:2 * gw])
    v_hi, v_lo = _split_bf16(v)
    vc = v - _dot(jnp.concatenate([v_hi, v_lo], axis=1), avg)
    sq_hi, sq_lo = _split_bf16(vc * vc)
    var = _dot(jnp.concatenate([sq_hi, sq_lo], axis=1), avg)
    vn = (vc * lax.rsqrt(var + EPS)).astype(BF16)
    rhs = jnp.concatenate(
        [_mask_heads(vn[c * CHUNK:(c + 1) * CHUNK], n_heads) for c in range(n_chunks)], axis=1)
    mix = _dot(wcat_ref[...], rhs)
    for c in range(n_chunks):
        r0 = c * CHUNK
        ya = u[r0:r0 + CHUNK] * (mix[:, c * gw:(c + 1) * gw] + sgub_ref[...])
        o_ref[r0:r0 + CHUNK, 0:gw] = _rms(ya, gain[:, 0:gw]).astype(o_ref.dtype)
    yield

    ki = lax.broadcasted_iota(jnp.int32, (1, 2 * CHUNK), 1)
    head_of_row = lax.broadcasted_iota(jnp.int32, (n_heads * CHUNK, 1), 0) // CHUNK
    sink = jnp.zeros((n_heads * CHUNK, 1), F32)
    for h in range(n_heads):
        sink = jnp.where(head_of_row == h, sink_ref[h], sink)
    bias = bias_ref[...]
    for c in range(n_chunks):
        r0 = c * CHUNK
        if c == 0:
            k_prev, v_prev = kh_ref[...], vh_ref[...]
        else:
            k_prev = pc_ref[r0 - CHUNK:r0, gw:2 * gw]
            v_prev = pc_ref[r0 - CHUNK:r0, 2 * gw:3 * gw]
        k2 = jnp.concatenate([k_prev, pc_ref[r0:r0 + CHUNK, gw:2 * gw]], axis=0)
        v2 = jnp.concatenate([v_prev, pc_ref[r0:r0 + CHUNK, 2 * gw:3 * gw]], axis=0)
        qm = _mask_heads(pc_ref[r0:r0 + CHUNK, 0:gw], n_heads)
        logits = _dot_nt(qm, k2) + bias
        if c == 0:
            logits = jnp.where(jnp.logical_or(ki >= CHUNK, jnp.logical_not(seq_start)),
                               logits, MASK_VALUE)
        mx = jnp.maximum(jnp.max(logits, axis=-1, keepdims=True), sink)
        e = jnp.exp(logits - mx)
        denom = jnp.sum(e, axis=-1, keepdims=True) + jnp.exp(sink - mx)
        probs = (e * (1.0 / denom)).astype(BF16)
        pcat = jnp.concatenate([probs[h * CHUNK:(h + 1) * CHUNK] for h in range(n_heads)], axis=1)
        yc = _dot(pcat, _mask_heads(v2, n_heads))
        o_ref[r0:r0 + CHUNK, 2 * gw:3 * gw] = _rms(yc, gain[:, 2 * gw:3 * gw]).astype(o_ref.dtype)
        yield


def _softplus(z):
    neg_abs = lax.bitcast_convert_type(
        lax.bitcast_convert_type(z, jnp.uint32) | jnp.uint32(0x80000000), F32)
    return jnp.maximum(z, 0.0) + jnp.log(1.0 + jnp.exp(neg_abs))


def _sb_attn_kernel(q_ref, k_ref, v_ref, upper_ref, gain_ref, o_ref,
                    qm_ref, acc_ref, carry_ref, cmin_ref):
    m = pl.program_id(1)
    n_sub = SB_SUBBLOCKS
    blk = q_ref.shape[0] // n_sub
    gw = q_ref.shape[1]
    half = gw // 2
    n_heads = gw // HEAD_DIM
    rows = n_heads * blk
    lane = lax.broadcasted_iota(jnp.int32, (1, gw), 1)
    lane_h = lax.broadcasted_iota(jnp.int32, (1, half), 1)

    for s in range(n_sub):
        q = q_ref[s * blk:(s + 1) * blk, :]
        for h in range(n_heads):
            qm_ref[s, h * blk:(h + 1) * blk, :] = jnp.where(lane // HEAD_DIM == h, q,
                                                            jnp.zeros_like(q))
    upper = upper_ref[...]
    causal = (lax.broadcasted_iota(jnp.int32, (blk, blk), 1)
              < lax.broadcasted_iota(jnp.int32, (blk, blk), 0))

    def keep_causal(x):
        return jnp.where(causal, x.reshape(n_heads, blk, blk), 0.0).reshape(rows, blk)

    def tile(s, t, carry, diag):
        start = pl.multiple_of((n_sub * m + s - t) * blk, blk)
        z = _dot_nt(qm_ref[s], k_ref[pl.ds(start, blk), :])
        sp = _softplus(z)
        log_sig = z - sp
        if diag:
            sp = keep_causal(sp)
        tail = _dot(sp.astype(BF16), upper)
        if carry is not None:
            tail = tail + jnp.concatenate([carry, carry], axis=1)
        w = jnp.exp(log_sig - tail)
        if diag:
            w = keep_causal(w)
        w = w.astype(BF16)
        rowsum = jnp.broadcast_to(jnp.sum(sp, axis=-1, keepdims=True), (rows, half))
        pv = []
        for p in range(n_heads // 2):
            r = _dot(w[2 * p * blk:(2 * p + 2) * blk],
                     v_ref[pl.ds(start, blk), p * half:(p + 1) * half])
            pv.append(jnp.where(lane_h < HEAD_DIM, r[:blk], r[blk:]))
        return pv, rowsum if carry is None else carry + rowsum

    def first_tiles(first_step):
        for s in range(n_sub):
            pv, carry = tile(s, 0, None, True)
            if s > 0 or not first_step:
                pv_b, carry = tile(s, 1, carry, False)
                pv = [a + b for a, b in zip(pv, pv_b)]
            carry_ref[s] = carry
            for p in range(n_heads // 2):
                acc_ref[s, p] = pv[p]
            cmin_ref[s] = jnp.min(carry)

    pl.when(m == 0)(functools.partial(first_tiles, True))
    pl.when(m > 0)(functools.partial(first_tiles, False))

    for s in range(n_sub):
        def more(state, s=s):
            t, cmin = state
            return jnp.logical_and(t <= n_sub * m + s, cmin <= EXP_UNDERFLOW)

        def visit(state, s=s):
            t, _ = state
            pv, carry = tile(s, t, carry_ref[s], False)
            carry_ref[s] = carry
            for p in range(n_heads // 2):
                acc_ref[s, p] += pv[p]
            return t + 1, jnp.min(carry)

        lax.while_loop(more, visit, (2, cmin_ref[s]))

    for s in range(n_sub):
        y = jnp.concatenate([acc_ref[s, p] for p in range(n_heads // 2)], axis=-1)
        o_ref[s * blk:(s + 1) * blk, :] = _rms(y, gain_ref[...]).astype(o_ref.dtype)


def _sb_attn(pd, upper, gain, batch, seq):
    m = pd.shape[0]
    gw = GROUP_WIDTH
    blk = SB_BLOCK
    n_sub = SB_SUBBLOCKS
    nq = seq // (n_sub * blk)
    n_heads = gw // HEAD_DIM
    whole = pl.BlockSpec(memory_space=pltpu.VMEM)
    return pl.pallas_call(
        _sb_attn_kernel,
        grid=(batch, nq),
        in_specs=[
            pl.BlockSpec((n_sub * blk, gw), lambda b, j: (b * nq + j, 0)),
            pl.BlockSpec((seq, gw), lambda b, j: (b, 1)),
            pl.BlockSpec((seq, gw), lambda b, j: (b, 2)),
            whole, whole,
        ],
        out_specs=pl.BlockSpec((n_sub * blk, gw), lambda b, j: (b * nq + j, 0)),
        out_shape=jax.ShapeDtypeStruct((m, gw), BF16),
        scratch_shapes=[pltpu.VMEM((n_sub, n_heads * blk, gw), BF16),
                        pltpu.VMEM((n_sub, n_heads // 2, blk, gw // 2), F32),
                        pltpu.VMEM((n_sub, n_heads * blk, gw // 2), F32),
                        pltpu.SMEM((n_sub,), F32)],
        compiler_params=pltpu.CompilerParams(
            dimension_semantics=("arbitrary", "arbitrary"), vmem_limit_bytes=VMEM_LIMIT_MIX),
        name="sb_attn",
    )(pd, pd, pd, upper, gain)


def _out_ffn_kernel(x_ref, yabc_ref, yd_ref, wout_ref, g_ref, wgu_ref, wdown_ref, gfin_ref,
                    o_ref, x2_ref, act_ref, *, final_norm):
    n_abc = yabc_ref.shape[1]
    d_ff = wdown_ref.shape[0]
    x2_ref[...] = (x_ref[...]
                   + _dot(yabc_ref[...], wout_ref[0:n_abc, :])
                   + _dot(yd_ref[...], wout_ref[n_abc:, :]))
    h = _rms(x2_ref[...], g_ref[...]).astype(BF16)
    for c in range(d_ff // FF_CHUNK):
        c0 = c * FF_CHUNK
        gate = _dot(h, wgu_ref[:, c0:c0 + FF_CHUNK])
        up = _dot(h, wgu_ref[:, d_ff + c0:d_ff + c0 + FF_CHUNK])
        act_ref[:, c0:c0 + FF_CHUNK] = (gate * (1.0 / (1.0 + jnp.exp(-gate))) * up).astype(BF16)
    out = x2_ref[...] + _dot(act_ref[...], wdown_ref[...])
    if final_norm:
        out = _rms(out, gfin_ref[...])
    o_ref[...] = out


def _out_ffn(x2d, yabc, yd, w_out, g, w_gu, w_down, g_final, layer, final_norm):
    m, d = x2d.shape
    tm = FFN_ROWS
    vec = pl.BlockSpec((1, d), lambda i: (0, 0))
    return pl.pallas_call(
        functools.partial(_out_ffn_kernel, final_norm=final_norm),
        grid=(m // tm,),
        in_specs=[
            pl.BlockSpec((tm, d), lambda i: (i, 0)),
            pl.BlockSpec((tm, yabc.shape[1]), lambda i: (i, 0)),
            pl.BlockSpec((tm, yd.shape[1]), lambda i: (i, 0)),
            _layer_weight(w_out, layer), vec,
            _layer_weight(w_gu, layer), _layer_weight(w_down, layer), vec,
        ],
        out_specs=pl.BlockSpec((tm, d), lambda i: (i, 0)),
        out_shape=jax.ShapeDtypeStruct((m, d), F32),
        scratch_shapes=[pltpu.VMEM((tm, d), F32), pltpu.VMEM((tm, w_down.shape[1]), BF16)],
        compiler_params=pltpu.CompilerParams(
            dimension_semantics=("arbitrary",), vmem_limit_bytes=VMEM_LIMIT_DENSE),
        name="out_ffn",
    )(x2d, yabc, yd, w_out, g, w_gu, w_down, g_final)


def _t5_bucket(dist):
    max_exact = N_BUCKETS // 2
    df = jnp.maximum(dist, 1).astype(F32)
    large = max_exact + (jnp.log(df / max_exact) / math.log(MAX_DISTANCE / max_exact)
                         * (N_BUCKETS - max_exact)).astype(jnp.int32)
    large = jnp.minimum(large, N_BUCKETS - 1)
    return jnp.where(dist < max_exact, dist, large)


def kernel(x, w_in, w_out, sgu_w, sgu_b, pool_w, pool_scale, swa_sinks, rel_bias, mix_out_gain,
           norm_mix, norm_ffn, w_gate_up, w_down, norm_final):
    batch, seq, d = x.shape
    depth = w_in.shape[0]
    gw = GROUP_WIDTH
    n_pool = len(POOL_WINDOWS)
    rows = batch * seq
    assert seq % (SB_BLOCK * SB_SUBBLOCKS) == 0
    assert rows % PROJ_ROWS == 0 and rows % FFN_ROWS == 0

    head_id = np.arange(gw) // HEAD_DIM
    avg = (head_id[:, None] == head_id[None, :]) / HEAD_DIM
    avg2 = jnp.asarray(np.concatenate([avg, avg], axis=0), BF16)
    upper = jnp.asarray(np.arange(SB_BLOCK)[:, None] > np.arange(SB_BLOCK)[None, :], BF16)
    dist = (jnp.arange(CHUNK)[:, None] + CHUNK) - jnp.arange(2 * CHUNK)[None, :]
    bucket = _t5_bucket(jnp.clip(dist, 0, CHUNK - 1)).astype(jnp.int32)

    w_in_b, w_out_b = w_in.astype(BF16), w_out.astype(BF16)
    w_gu_b, w_down_b = w_gate_up.astype(BF16), w_down.astype(BF16)
    pool_bd = (pool_w[:, :, :, None, :] * jnp.eye(n_pool, dtype=F32)[None, :, None, :, None])
    pool_bd = pool_bd.reshape(depth, gw, gw).astype(BF16)
    sgu_b_tile = jnp.repeat(jnp.swapaxes(sgu_b, 1, 2), HEAD_DIM, axis=2)
    gains = mix_out_gain.reshape(depth, 1, 4 * gw)

    x2d = x.reshape(rows, d)
    for l in range(depth):
        yabc, pd = _proj_mix(x2d, norm_mix[l].reshape(1, d), w_in_b, l, rel_bias, swa_sinks[l],
                             sgu_w[l], sgu_b_tile[l], avg2, pool_bd[l], pool_scale[l].reshape(1, gw),
                             bucket, gains[l, :, :3 * gw], seq)
        yd = _sb_attn(pd, upper, gains[l, :, 3 * gw:], batch, seq)
        x2d = _out_ffn(x2d, yabc, yd, w_out_b, norm_ffn[l].reshape(1, d), w_gu_b, w_down_b,
                       norm_final.reshape(1, d), l, l == depth - 1)
    return x2d.reshape(batch, seq, d)
```

```python
import functools
import math

import numpy as np
import jax
import jax.numpy as jnp
from jax import lax
from jax.experimental import pallas as pl
from jax.experimental.pallas import tpu as pltpu

F32 = jnp.float32
BF16 = jnp.bfloat16

HEAD_DIM = 64
GROUP_WIDTH = 256
CHUNK = 128
POOL_WINDOWS = (2, 4, 8, 16)
POOL_HALO = 16
N_BUCKETS = 32
MAX_DISTANCE = 128
EPS = 1e-6
MASK_VALUE = -1e30
EXP_UNDERFLOW = 104.0

SB_BLOCK = 256
SB_SUBBLOCKS = 8
PROJ_ROWS = 2048
PROJ_PARTS = 2
FFN_ROWS = 1024
FF_CHUNK = 256
VMEM_LIMIT_DENSE = 56 * 1024 * 1024
VMEM_LIMIT_MIX = 40 * 1024 * 1024


def _rms(x, g):
    return x * lax.rsqrt(jnp.mean(x * x, axis=-1, keepdims=True) + EPS) * g


def _gelu_tanh(x):
    inner = math.sqrt(2.0 / math.pi) * (x + 0.044715 * (x * x * x))
    return x * (0.5 * (1.0 + jnp.tanh(inner)))


def _split_bf16(x):
    hi = x.astype(BF16)
    lo = (x - hi.astype(F32)).astype(BF16)
    return hi, lo


def _dot(a, b):
    return jnp.dot(a, b, preferred_element_type=F32)


def _dot_nt(a, b):
    return lax.dot_general(a, b, (((1,), (1,)), ((), ())), preferred_element_type=F32)


def _layer_weight(w, layer):
    return pl.BlockSpec((None,) + w.shape[1:], lambda *_: (layer, 0, 0),
                        pipeline_mode=pl.Buffered(1))


def _per_query_head(kv):
    lane = lax.broadcasted_iota(jnp.int32, (1, kv.shape[1]), 1)
    swapped = pltpu.roll(kv, HEAD_DIM, 1)
    first = jnp.where(lane < HEAD_DIM, kv, swapped)
    second = jnp.where(lane < HEAD_DIM, swapped, kv)
    return jnp.concatenate([first, second], axis=1)


def _project(h, w_ref, pa_ref, pc_ref, pd_ref, r0, side_work):
    gw = GROUP_WIDTH
    rows = h.shape[0]
    kvw = w_ref.shape[1] - 7 * gw
    scale = HEAD_DIM ** -0.5
    c_k = 4 * gw
    c_v = c_k + kvw // 2
    d_q = c_v + kvw // 2
    sl = slice(r0, r0 + rows)

    def cols(lo, hi):
        return _dot(h, w_ref[:, lo:hi])

    pa_ref[sl, 0:gw] = cols(0, gw)
    next(side_work, None)
    next(side_work, None)
    pa_ref[sl, gw:2 * gw] = cols(gw, 2 * gw)
    next(side_work, None)
    next(side_work, None)
    pa_ref[sl, 2 * gw:3 * gw] = cols(2 * gw, 3 * gw)
    next(side_work, None)
    next(side_work, None)
    pc_ref[sl, 0:gw] = (cols(3 * gw, 4 * gw) * scale).astype(BF16)
    next(side_work, None)
    next(side_work, None)
    kv = cols(c_k, d_q)
    pc_ref[sl, gw:2 * gw] = _per_query_head(kv[:, :kvw // 2]).astype(BF16)
    pc_ref[sl, 2 * gw:3 * gw] = _per_query_head(kv[:, kvw // 2:]).astype(BF16)
    next(side_work, None)
    next(side_work, None)
    pd_ref[sl, 0:gw] = (cols(d_q, d_q + gw) * scale).astype(BF16)
    next(side_work, None)
    next(side_work, None)
    pd_ref[sl, gw:3 * gw] = cols(d_q + gw, d_q + 3 * gw).astype(BF16)


def _proj_mix_kernel(relb_ref, sink_ref,
                     x_ref, g_ref, w_ref,
                     sguw_ref, sgub_ref, avg_ref, poolw_ref, pscale_ref, bucket_ref, gain_ref,
                     y_ref, pd_ref,
                     pa_ref, pc_ref, hb_ref, hk_ref, hv_ref, bias_ref, wcat_ref,
                     *, blocks_per_seq):
    i = pl.program_id(0)
    rows = x_ref.shape[0]
    part = rows // PROJ_PARTS
    gw = GROUP_WIDTH
    seq_start = i % blocks_per_seq == 0
    t0 = (i % blocks_per_seq) * rows

    @pl.when(i == 0)
    def _():
        _local_mix_tables(relb_ref, sguw_ref, bucket_ref, bias_ref, wcat_ref)
        hb_ref[...] = jnp.zeros_like(hb_ref)
        hk_ref[...] = jnp.zeros_like(hk_ref)
        hv_ref[...] = jnp.zeros_like(hv_ref)

    h = _rms(x_ref[...], g_ref[...]).astype(BF16)

    def mixers(p):
        r0 = p * part
        if p == 0:
            first, halos = seq_start, (hb_ref, hk_ref, hv_ref)
        else:
            first = False
            halos = (pa_ref.at[r0 - POOL_HALO:r0, 2 * gw:3 * gw],
                     pc_ref.at[r0 - CHUNK:r0, gw:2 * gw],
                     pc_ref.at[r0 - CHUNK:r0, 2 * gw:3 * gw])
        return _local_mix_pieces(
            first, t0 + r0, sink_ref,
            pa_ref.at[r0:r0 + part], halos[0], pc_ref.at[r0:r0 + part], halos[1], halos[2],
            sgub_ref, avg_ref, poolw_ref, pscale_ref, gain_ref,
            y_ref.at[r0:r0 + part], bias_ref, wcat_ref)

    side_work = iter(())
    for p in range(PROJ_PARTS):
        _project(h[p * part:(p + 1) * part], w_ref, pa_ref, pc_ref, pd_ref, p * part, side_work)
        for _ in side_work:
            pass
        side_work = mixers(p)
    for _ in side_work:
        pass
    hb_ref[...] = pa_ref[rows - POOL_HALO:rows, 2 * gw:3 * gw]
    hk_ref[...] = pc_ref[rows - CHUNK:rows, gw:2 * gw]
    hv_ref[...] = pc_ref[rows - CHUNK:rows, 2 * gw:3 * gw]


def _proj_mix(x2d, g, w, layer, rel_bias, sinks, sgu_w, sgu_b_tile, avg, pool_bd, pool_scale,
              bucket, gain, seq):
    m, d = x2d.shape
    tm = PROJ_ROWS
    gw = GROUP_WIDTH
    n_heads = gw // HEAD_DIM
    assert w.shape[2] == 7 * gw + 4 * HEAD_DIM, "expects 2 kv heads of HEAD_DIM"
    assert seq % tm == 0 and (tm // PROJ_PARTS) % CHUNK == 0
    smem = pl.BlockSpec(memory_space=pltpu.SMEM)
    whole = pl.BlockSpec(memory_space=pltpu.VMEM)
    return pl.pallas_call(
        functools.partial(_proj_mix_kernel, blocks_per_seq=seq // tm),
        grid=(m // tm,),
        in_specs=[
            smem, smem,
            pl.BlockSpec((tm, d), lambda i: (i, 0)),
            pl.BlockSpec((1, d), lambda i: (0, 0)),
            _layer_weight(w, layer),
            whole, whole, whole, whole, whole, whole, whole,
        ],
        out_specs=[
            pl.BlockSpec((tm, 3 * gw), lambda i: (i, 0)),
            pl.BlockSpec((tm, 3 * gw), lambda i: (i, 0)),
        ],
        out_shape=[
            jax.ShapeDtypeStruct((m, 3 * gw), BF16),
            jax.ShapeDtypeStruct((m, 3 * gw), BF16),
        ],
        scratch_shapes=[pltpu.VMEM((tm, 3 * gw), F32),
                        pltpu.VMEM((tm, 3 * gw), BF16),
                        pltpu.VMEM((POOL_HALO, gw), F32),
                        pltpu.VMEM((CHUNK, gw), BF16),
                        pltpu.VMEM((CHUNK, gw), BF16),
                        pltpu.VMEM((n_heads * CHUNK, 2 * CHUNK), F32),
                        pltpu.VMEM((CHUNK, n_heads * CHUNK), BF16)],
        compiler_params=pltpu.CompilerParams(
            dimension_semantics=("arbitrary",), vmem_limit_bytes=VMEM_LIMIT_DENSE),
        name="proj_mix",
    )(rel_bias, sinks, x2d, g, w, sgu_w, sgu_b_tile, avg, pool_bd, pool_scale, bucket, gain)


def _mask_heads(x, n_heads):
    lane = lax.broadcasted_iota(jnp.int32, (1, x.shape[1]), 1)
    return jnp.concatenate(
        [jnp.where(lane // HEAD_DIM == h, x, jnp.zeros_like(x)) for h in range(n_heads)], axis=0)


def _local_mix_tables(relb_ref, sguw_ref, bucket_ref, bias_ref, wcat_ref):
    n_heads = sguw_ref.shape[0]
    bucket = bucket_ref[...]
    tril = (lax.broadcasted_iota(jnp.int32, (CHUNK, CHUNK), 1)
            <= lax.broadcasted_iota(jnp.int32, (CHUNK, CHUNK), 0))
    qi = lax.broadcasted_iota(jnp.int32, bucket.shape, 0)
    ki = lax.broadcasted_iota(jnp.int32, bucket.shape, 1)
    in_window = jnp.logical_and(ki > qi, ki <= qi + CHUNK)
    for h in range(n_heads):
        acc = jnp.zeros(bucket.shape, F32)
        for bk in range(N_BUCKETS):
            acc = jnp.where(bucket == bk, relb_ref[bk, h], acc)
        bias_ref[h * CHUNK:(h + 1) * CHUNK, :] = jnp.where(in_window, acc, MASK_VALUE)
        wcat_ref[:, h * CHUNK:(h + 1) * CHUNK] = jnp.where(tril, sguw_ref[h], 0.0).astype(BF16)


def _local_mix_pieces(seq_start, t0, sink_ref,
                      pa_ref, halo_ref, pc_ref, kh_ref, vh_ref,
                      sgub_ref, avg_ref, poolw_ref, pscale_ref, gain_ref,
                      o_ref, bias_ref, wcat_ref):
    rows = pa_ref.shape[0]
    gw = GROUP_WIDTH
    n_heads = gw // HEAD_DIM
    n_chunks = rows // CHUNK
    lane = lax.broadcasted_iota(jnp.int32, (1, gw), 1)
    gain = gain_ref[...]

    p = pa_ref[:, 2 * gw:3 * gw]
    halo = jnp.where(seq_start, 0.0, halo_ref[...])
    ext = jnp.concatenate([halo, p], axis=0)
    sums = []
    run = ext
    for shift in (1, 2, 4, 8):
        run = run + pltpu.roll(run, shift, 0)
        sums.append(run[POOL_HALO:])
    group = lane // (gw // len(POOL_WINDOWS))
    wsum = jnp.where(group == 0, sums[0],
                     jnp.where(group == 1, sums[1],
                               jnp.where(group == 2, sums[2], sums[3])))
    win = jnp.where(group == 0, POOL_WINDOWS[0],
                    jnp.where(group == 1, POOL_WINDOWS[1],
                              jnp.where(group == 2, POOL_WINDOWS[2], POOL_WINDOWS[3])))
    t_seq = t0 + lax.broadcasted_iota(jnp.int32, (rows, 1), 0)
    count = jnp.minimum(t_seq + 1, win).astype(F32)
    yb = wsum / count - p
    yb = _dot(yb.astype(BF16), poolw_ref[...]) * pscale_ref[...]
    o_ref[:, gw:2 * gw] = _rms(yb, gain[:, gw:2 * gw]).astype(o_ref.dtype)
    yield

    avg = avg_ref[...]
    u = _gelu_tanh(pa_ref[:, 0:gw])
    v = _gelu_tanh(pa_ref[:, gw:2 * gw])
    v_hi, v_lo = _split_bf16(v)
    vc = v - _dot(jnp.concatenate([v_hi, v_lo], axis=1), avg)
    sq_hi, sq_lo = _split_bf16(vc * vc)
    var = _dot(jnp.concatenate([sq_hi, sq_lo], axis=1), avg)
    vn = (vc * lax.rsqrt(var + EPS)).astype(BF16)
    rhs = jnp.concatenate(
        [_mask_heads(vn[c * CHUNK:(c + 1) * CHUNK], n_heads) for c in range(n_chunks)], axis=1)
    mix = _dot(wcat_ref[...], rhs)
    for c in range(n_chunks):
        r0 = c * CHUNK
        ya = u[r0:r0 + CHUNK] * (mix[:, c * gw:(c + 1) * gw] + sgub_ref[...])
        o_ref[r0:r0 + CHUNK, 0:gw] = _rms(ya, gain[:, 0:gw]).astype(o_ref.dtype)
    yield

    ki = lax.broadcasted_iota(jnp.int32, (1, 2 * CHUNK), 1)
    head_of_row = lax.broadcasted_iota(jnp.int32, (n_heads * CHUNK, 1), 0) // CHUNK
    sink = jnp.zeros((n_heads * CHUNK, 1), F32)
    for h in range(n_heads):
        sink = jnp.where(head_of_row == h, sink_ref[h], sink)
    bias = bias_ref[...]
    for c in range(n_chunks):
        r0 = c * CHUNK
        if c == 0:
            k_prev, v_prev = kh_ref[...], vh_ref[...]
        else:
            k_prev = pc_ref[r0 - CHUNK:r0, gw:2 * gw]
            v_prev = pc_ref[r0 - CHUNK:r0, 2 * gw:3 * gw]
        k2 = jnp.concatenate([k_prev, pc_ref[r0:r0 + CHUNK, gw:2 * gw]], axis=0)
        v2 = jnp.concatenate([v_prev, pc_ref[r0:r0 + CHUNK, 2 * gw:3 * gw]], axis=0)
        qm = _mask_heads(pc_ref[r0:r0 + CHUNK, 0:gw], n_heads)
        logits = _dot_nt(qm, k2) + bias
        if c == 0:
            logits = jnp.where(jnp.logical_or(ki >= CHUNK, jnp.logical_not(seq_start)),
                               logits, MASK_VALUE)
        mx = jnp.maximum(jnp.max(logits, axis=-1, keepdims=True), sink)
        e = jnp.exp(logits - mx)
        denom = jnp.sum(e, axis=-1, keepdims=True) + jnp.exp(sink - mx)
        probs = (e * (1.0 / denom)).astype(BF16)
        pcat = jnp.concatenate([probs[h * CHUNK:(h + 1) * CHUNK] for h in range(n_heads)], axis=1)
        yc = _dot(pcat, _mask_heads(v2, n_heads))
        o_ref[r0:r0 + CHUNK, 2 * gw:3 * gw] = _rms(yc, gain[:, 2 * gw:3 * gw]).astype(o_ref.dtype)
        yield


def _softplus(z):
    neg_abs = lax.bitcast_convert_type(
        lax.bitcast_convert_type(z, jnp.uint32) | jnp.uint32(0x80000000), F32)
    return jnp.maximum(z, 0.0) + jnp.log(1.0 + jnp.exp(neg_abs))


def _sb_attn_kernel(q_ref, k_ref, v_ref, upper_ref, gain_ref, o_ref,
                    qm_ref, acc_ref, carry_ref, cmin_ref):
    m = pl.program_id(1)
    n_sub = SB_SUBBLOCKS
    blk = q_ref.shape[0] // n_sub
    gw = q_ref.shape[1]
    half = gw // 2
    n_heads = gw // HEAD_DIM
    rows = n_heads * blk
    lane = lax.broadcasted_iota(jnp.int32, (1, gw), 1)
    lane_h = lax.broadcasted_iota(jnp.int32, (1, half), 1)

    for s in range(n_sub):
        q = q_ref[s * blk:(s + 1) * blk, :]
        for h in range(n_heads):
            qm_ref[s, h * blk:(h + 1) * blk, :] = jnp.where(lane // HEAD_DIM == h, q,
                                                            jnp.zeros_like(q))
    upper = upper_ref[...]
    causal = (lax.broadcasted_iota(jnp.int32, (blk, blk), 1)
              < lax.broadcasted_iota(jnp.int32, (blk, blk), 0))

    def keep_causal(x):
        return jnp.where(causal, x.reshape(n_heads, blk, blk), 0.0).reshape(rows, blk)

    def tile(s, t, carry, diag):
        start = pl.multiple_of((n_sub * m + s - t) * blk, blk)
        z = _dot_nt(qm_ref[s], k_ref[pl.ds(start, blk), :])
        sp = _softplus(z)
        log_sig = z - sp
        if diag:
            sp = keep_causal(sp)
        tail = _dot(sp.astype(BF16), upper)
        if carry is not None:
            tail = tail + jnp.concatenate([carry, carry], axis=1)
        w = jnp.exp(log_sig - tail)
        if diag:
            w = keep_causal(w)
        w = w.astype(BF16)
        rowsum = jnp.broadcast_to(jnp.sum(sp, axis=-1, keepdims=True), (rows, half))
        pv = []
        for p in range(n_heads // 2):
            r = _dot(w[2 * p * blk:(2 * p + 2) * blk],
                     v_ref[pl.ds(start, blk), p * half:(p + 1) * half])
            pv.append(jnp.where(lane_h < HEAD_DIM, r[:blk], r[blk:]))
        return pv, rowsum if carry is None else carry + rowsum

    def first_tiles(first_step):
        for s in range(n_sub):
            pv, carry = tile(s, 0, None, True)
            if s > 0 or not first_step:
                pv_b, carry = tile(s, 1, carry, False)
                pv = [a + b for a, b in zip(pv, pv_b)]
            carry_ref[s] = carry
            for p in range(n_heads // 2):
                acc_ref[s, p] = pv[p]
            cmin_ref[s] = jnp.min(carry)

    pl.when(m == 0)(functools.partial(first_tiles, True))
    pl.when(m > 0)(functools.partial(first_tiles, False))

    for s in range(n_sub):
        def more(state, s=s):
            t, cmin = state
            return jnp.logical_and(t <= n_sub * m + s, cmin <= EXP_UNDERFLOW)

        def visit(state, s=s):
            t, _ = state
            pv, carry = tile(s, t, carry_ref[s], False)
            carry_ref[s] = carry
            for p in range(n_heads // 2):
                acc_ref[s, p] += pv[p]
            return t + 1, jnp.min(carry)

        lax.while_loop(more, visit, (2, cmin_ref[s]))

    for s in range(n_sub):
        y = jnp.concatenate([acc_ref[s, p] for p in range(n_heads // 2)], axis=-1)
        o_ref[s * blk:(s + 1) * blk, :] = _rms(y, gain_ref[...]).astype(o_ref.dtype)


def _sb_attn(pd, upper, gain, batch, seq):
    m = pd.shape[0]
    gw = GROUP_WIDTH
    blk = SB_BLOCK
    n_sub = SB_SUBBLOCKS
    nq = seq // (n_sub * blk)
    n_heads = gw // HEAD_DIM
    whole = pl.BlockSpec(memory_space=pltpu.VMEM)
    return pl.pallas_call(
        _sb_attn_kernel,
        grid=(batch, nq),
        in_specs=[
            pl.BlockSpec((n_sub * blk, gw), lambda b, j: (b * nq + j, 0)),
            pl.BlockSpec((seq, gw), lambda b, j: (b, 1)),
            pl.BlockSpec((seq, gw), lambda b, j: (b, 2)),
            whole, whole,
        ],
        out_specs=pl.BlockSpec((n_sub * blk, gw), lambda b, j: (b * nq + j, 0)),
        out_shape=jax.ShapeDtypeStruct((m, gw), BF16),
        scratch_shapes=[pltpu.VMEM((n_sub, n_heads * blk, gw), BF16),
                        pltpu.VMEM((n_sub, n_heads // 2, blk, gw // 2), F32),
                        pltpu.VMEM((n_sub, n_heads * blk, gw // 2), F32),
                        pltpu.SMEM((n_sub,), F32)],
        compiler_params=pltpu.CompilerParams(
            dimension_semantics=("arbitrary", "arbitrary"), vmem_limit_bytes=VMEM_LIMIT_MIX),
        name="sb_attn",
    )(pd, pd, pd, upper, gain)


def _out_ffn_kernel(x_ref, yabc_ref, yd_ref, wout_ref, g_ref, wgu_ref, wdown_ref, gfin_ref,
                    o_ref, x2_ref, act_ref, *, final_norm):
    n_abc = yabc_ref.shape[1]
    d_ff = wdown_ref.shape[0]
    x2_ref[...] = (x_ref[...]
                   + _dot(yabc_ref[...], wout_ref[0:n_abc, :])
                   + _dot(yd_ref[...], wout_ref[n_abc:, :]))
    h = _rms(x2_ref[...], g_ref[...]).astype(BF16)
    for c in range(d_ff // FF_CHUNK):
        c0 = c * FF_CHUNK
        gate = _dot(h, wgu_ref[:, c0:c0 + FF_CHUNK])
        up = _dot(h, wgu_ref[:, d_ff + c0:d_ff + c0 + FF_CHUNK])
        act_ref[:, c0:c0 + FF_CHUNK] = (gate * (1.0 / (1.0 + jnp.exp(-gate))) * up).astype(BF16)
    out = x2_ref[...] + _dot(act_ref[...], wdown_ref[...])
    if final_norm:
        out = _rms(out, gfin_ref[...])
    o_ref[...] = out


def _out_ffn(x2d, yabc, yd, w_out, g, w_gu, w_down, g_final, layer, final_norm):
    m, d = x2d.shape
    tm = FFN_ROWS
    vec = pl.BlockSpec((1, d), lambda i: (0, 0))
    return pl.pallas_call(
        functools.partial(_out_ffn_kernel, final_norm=final_norm),
        grid=(m // tm,),
        in_specs=[
            pl.BlockSpec((tm, d), lambda i: (i, 0)),
            pl.BlockSpec((tm, yabc.shape[1]), lambda i: (i, 0)),
            pl.BlockSpec((tm, yd.shape[1]), lambda i: (i, 0)),
            _layer_weight(w_out, layer), vec,
            _layer_weight(w_gu, layer), _layer_weight(w_down, layer), vec,
        ],
        out_specs=pl.BlockSpec((tm, d), lambda i: (i, 0)),
        out_shape=jax.ShapeDtypeStruct((m, d), F32),
        scratch_shapes=[pltpu.VMEM((tm, d), F32), pltpu.VMEM((tm, w_down.shape[1]), BF16)],
        compiler_params=pltpu.CompilerParams(
            dimension_semantics=("arbitrary",), vmem_limit_bytes=VMEM_LIMIT_DENSE),
        name="out_ffn",
    )(x2d, yabc, yd, w_out, g, w_gu, w_down, g_final)


def _t5_bucket(dist):
    max_exact = N_BUCKETS // 2
    df = jnp.maximum(dist, 1).astype(F32)
    large = max_exact + (jnp.log(df / max_exact) / math.log(MAX_DISTANCE / max_exact)
                         * (N_BUCKETS - max_exact)).astype(jnp.int32)
    large = jnp.minimum(large, N_BUCKETS - 1)
    return jnp.where(dist < max_exact, dist, large)


def kernel(x, w_in, w_out, sgu_w, sgu_b, pool_w, pool_scale, swa_sinks, rel_bias, mix_out_gain,
           norm_mix, norm_ffn, w_gate_up, w_down, norm_final):
    batch, seq, d = x.shape
    depth = w_in.shape[0]
    gw = GROUP_WIDTH
    n_pool = len(POOL_WINDOWS)
    rows = batch * seq
    assert seq % (SB_BLOCK * SB_SUBBLOCKS) == 0
    assert rows % PROJ_ROWS == 0 and rows % FFN_ROWS == 0

    head_id = np.arange(gw) // HEAD_DIM
    avg = (head_id[:, None] == head_id[None, :]) / HEAD_DIM
    avg2 = jnp.asarray(np.concatenate([avg, avg], axis=0), BF16)
    upper = jnp.asarray(np.arange(SB_BLOCK)[:, None] > np.arange(SB_BLOCK)[None, :], BF16)
    dist = (jnp.arange(CHUNK)[:, None] + CHUNK) - jnp.arange(2 * CHUNK)[None, :]
    bucket = _t5_bucket(jnp.clip(dist, 0, CHUNK - 1)).astype(jnp.int32)

    w_in_b, w_out_b = w_in.astype(BF16), w_out.astype(BF16)
    w_gu_b, w_down_b = w_gate_up.astype(BF16), w_down.astype(BF16)
    pool_bd = (pool_w[:, :, :, None, :] * jnp.eye(n_pool, dtype=F32)[None, :, None, :, None])
    pool_bd = pool_bd.reshape(depth, gw, gw).astype(BF16)
    sgu_b_tile = jnp.repeat(jnp.swapaxes(sgu_b, 1, 2), HEAD_DIM, axis=2)
    gains = mix_out_gain.reshape(depth, 1, 4 * gw)

    x2d = x.reshape(rows, d)
    for l in range(depth):
        yabc, pd = _proj_mix(x2d, norm_mix[l].reshape(1, d), w_in_b, l, rel_bias, swa_sinks[l],
                             sgu_w[l], sgu_b_tile[l], avg2, pool_bd[l], pool_scale[l].reshape(1, gw),
                             bucket, gains[l, :, :3 * gw], seq)
        yd = _sb_attn(pd, upper, gains[l, :, 3 * gw:], batch, seq)
        x2d = _out_ffn(x2d, yabc, yd, w_out_b, norm_ffn[l].reshape(1, d), w_gu_b, w_down_b,
                       norm_final.reshape(1, d), l, l == depth - 1)
    return x2d.reshape(batch, seq, d)
```

```python
import functools
import math

import numpy as np
import jax
import jax.numpy as jnp
from jax import lax
from jax.experimental import pallas as pl
from jax.experimental.pallas import tpu as pltpu

F32 = jnp.float32
BF16 = jnp.bfloat16

HEAD_DIM = 64
GROUP_WIDTH = 256
CHUNK = 128
POOL_WINDOWS = (2, 4, 8, 16)
POOL_HALO = 16
N_BUCKETS = 32
MAX_DISTANCE = 128
EPS = 1e-6
MASK_VALUE = -1e30
EXP_UNDERFLOW = 104.0

SB_BLOCK = 256
SB_SUBBLOCKS = 8
PROJ_ROWS = 2048
PROJ_PARTS = 2
FFN_ROWS = 1024
FF_CHUNK = 256
VMEM_LIMIT_DENSE = 56 * 1024 * 1024
VMEM_LIMIT_MIX = 40 * 1024 * 1024


def _rms(x, g):
    return x * lax.rsqrt(jnp.mean(x * x, axis=-1, keepdims=True) + EPS) * g


def _gelu_tanh(x):
    inner = math.sqrt(2.0 / math.pi) * (x + 0.044715 * (x * x * x))
    return x * (0.5 * (1.0 + jnp.tanh(inner)))


def _split_bf16(x):
    hi = x.astype(BF16)
    lo = (x - hi.astype(F32)).astype(BF16)
    return hi, lo


def _dot(a, b):
    return jnp.dot(a, b, preferred_element_type=F32)


def _dot_nt(a, b):
    return lax.dot_general(a, b, (((1,), (1,)), ((), ())), preferred_element_type=F32)


def _layer_weight(w, layer):
    return pl.BlockSpec((None,) + w.shape[1:], lambda *_: (layer, 0, 0),
                        pipeline_mode=pl.Buffered(1))


def _per_query_head(kv):
    lane = lax.broadcasted_iota(jnp.int32, (1, kv.shape[1]), 1)
    swapped = pltpu.roll(kv, HEAD_DIM, 1)
    first = jnp.where(lane < HEAD_DIM, kv, swapped)
    second = jnp.where(lane < HEAD_DIM, swapped, kv)
    return jnp.concatenate([first, second], axis=1)


def _project(h, w_ref, pa_ref, pc_ref, pd_ref, r0, side_work):
    gw = GROUP_WIDTH
    rows = h.shape[0]
    kvw = w_ref.shape[1] - 7 * gw
    scale = HEAD_DIM ** -0.5
    c_k = 4 * gw
    c_v = c_k + kvw // 2
    d_q = c_v + kvw // 2
    sl = slice(r0, r0 + rows)

    def cols(lo, hi):
        return _dot(h, w_ref[:, lo:hi])

    pa_ref[sl, 0:gw] = cols(0, gw)
    next(side_work, None)
    next(side_work, None)
    pa_ref[sl, gw:2 * gw] = cols(gw, 2 * gw)
    next(side_work, None)
    next(side_work, None)
    pa_ref[sl, 2 * gw:3 * gw] = cols(2 * gw, 3 * gw)
    next(side_work, None)
    next(side_work, None)
    pc_ref[sl, 0:gw] = (cols(3 * gw, 4 * gw) * scale).astype(BF16)
    next(side_work, None)
    next(side_work, None)
    kv = cols(c_k, d_q)
    pc_ref[sl, gw:2 * gw] = _per_query_head(kv[:, :kvw // 2]).astype(BF16)
    pc_ref[sl, 2 * gw:3 * gw] = _per_query_head(kv[:, kvw // 2:]).astype(BF16)
    next(side_work, None)
    next(side_work, None)
    pd_ref[sl, 0:gw] = (cols(d_q, d_q + gw) * scale).astype(BF16)
    next(side_work, None)
    next(side_work, None)
    pd_ref[sl, gw:3 * gw] = cols(d_q + gw, d_q + 3 * gw).astype(BF16)


def _proj_mix_kernel(relb_ref, sink_ref,
                     x_ref, g_ref, w_ref,
                     sguw_ref, sgub_ref, avg_ref, poolw_ref, pscale_ref, bucket_ref, gain_ref,
                     y_ref, pd_ref,
                     pa_ref, pc_ref, hb_ref, hk_ref, hv_ref, bias_ref, wcat_ref,
                     *, blocks_per_seq):
    i = pl.program_id(0)
    rows = x_ref.shape[0]
    part = rows // PROJ_PARTS
    gw = GROUP_WIDTH
    seq_start = i % blocks_per_seq == 0
    t0 = (i % blocks_per_seq) * rows

    @pl.when(i == 0)
    def _():
        _local_mix_tables(relb_ref, sguw_ref, bucket_ref, bias_ref, wcat_ref)
        hb_ref[...] = jnp.zeros_like(hb_ref)
        hk_ref[...] = jnp.zeros_like(hk_ref)
        hv_ref[...] = jnp.zeros_like(hv_ref)

    h = _rms(x_ref[...], g_ref[...]).astype(BF16)

    def mixers(p):
        r0 = p * part
        if p == 0:
            first, halos = seq_start, (hb_ref, hk_ref, hv_ref)
        else:
            first = False
            halos = (pa_ref.at[r0 - POOL_HALO:r0, 2 * gw:3 * gw],
                     pc_ref.at[r0 - CHUNK:r0, gw:2 * gw],
                     pc_ref.at[r0 - CHUNK:r0, 2 * gw:3 * gw])
        return _local_mix_pieces(
            first, t0 + r0, sink_ref,
            pa_ref.at[r0:r0 + part], halos[0], pc_ref.at[r0:r0 + part], halos[1], halos[2],
            sgub_ref, avg_ref, poolw_ref, pscale_ref, gain_ref,
            y_ref.at[r0:r0 + part], bias_ref, wcat_ref)

    side_work = iter(())
    for p in range(PROJ_PARTS):
        _project(h[p * part:(p + 1) * part], w_ref, pa_ref, pc_ref, pd_ref, p * part, side_work)
        for _ in side_work:
            pass
        side_work = mixers(p)
    for _ in side_work:
        pass
    hb_ref[...] = pa_ref[rows - POOL_HALO:rows, 2 * gw:3 * gw]
    hk_ref[...] = pc_ref[rows - CHUNK:rows, gw:2 * gw]
    hv_ref[...] = pc_ref[rows - CHUNK:rows, 2 * gw:3 * gw]


def _proj_mix(x2d, g, w, layer, rel_bias, sinks, sgu_w, sgu_b_tile, avg, pool_bd, pool_scale,
              bucket, gain, seq):
    m, d = x2d.shape
    tm = PROJ_ROWS
    gw = GROUP_WIDTH
    n_heads = gw // HEAD_DIM
    assert w.shape[2] == 7 * gw + 4 * HEAD_DIM, "expects 2 kv heads of HEAD_DIM"
    assert seq % tm == 0 and (tm // PROJ_PARTS) % CHUNK == 0
    smem = pl.BlockSpec(memory_space=pltpu.SMEM)
    whole = pl.BlockSpec(memory_space=pltpu.VMEM)
    return pl.pallas_call(
        functools.partial(_proj_mix_kernel, blocks_per_seq=seq // tm),
        grid=(m // tm,),
        in_specs=[
            smem, smem,
            pl.BlockSpec((tm, d), lambda i: (i, 0)),
            pl.BlockSpec((1, d), lambda i: (0, 0)),
            _layer_weight(w, layer),
            whole, whole, whole, whole, whole, whole, whole,
        ],
        out_specs=[
            pl.BlockSpec((tm, 3 * gw), lambda i: (i, 0)),
            pl.BlockSpec((tm, 3 * gw), lambda i: (i, 0)),
        ],
        out_shape=[
            jax.ShapeDtypeStruct((m, 3 * gw), BF16),
            jax.ShapeDtypeStruct((m, 3 * gw), BF16),
        ],
        scratch_shapes=[pltpu.VMEM((tm, 3 * gw), F32),
                        pltpu.VMEM((tm, 3 * gw), BF16),
                        pltpu.VMEM((POOL_HALO, gw), F32),
                        pltpu.VMEM((CHUNK, gw), BF16),
                        pltpu.VMEM((CHUNK, gw), BF16),
                        pltpu.VMEM((n_heads * CHUNK, 2 * CHUNK), F32),
                        pltpu.VMEM((CHUNK, n_heads * CHUNK), BF16)],
        compiler_params=pltpu.CompilerParams(
            dimension_semantics=("arbitrary",), vmem_limit_bytes=VMEM_LIMIT_DENSE),
        name="proj_mix",
    )(rel_bias, sinks, x2d, g, w, sgu_w, sgu_b_tile, avg, pool_bd, pool_scale, bucket, gain)


def _mask_heads(x, n_heads):
    lane = lax.broadcasted_iota(jnp.int32, (1, x.shape[1]), 1)
    return jnp.concatenate(
        [jnp.where(lane // HEAD_DIM == h, x, jnp.zeros_like(x)) for h in range(n_heads)], axis=0)


def _local_mix_tables(relb_ref, sguw_ref, bucket_ref, bias_ref, wcat_ref):
    n_heads = sguw_ref.shape[0]
    bucket = bucket_ref[...]
    tril = (lax.broadcasted_iota(jnp.int32, (CHUNK, CHUNK), 1)
            <= lax.broadcasted_iota(jnp.int32, (CHUNK, CHUNK), 0))
    qi = lax.broadcasted_iota(jnp.int32, bucket.shape, 0)
    ki = lax.broadcasted_iota(jnp.int32, bucket.shape, 1)
    in_window = jnp.logical_and(ki > qi, ki <= qi + CHUNK)
    for h in range(n_heads):
        acc = jnp.zeros(bucket.shape, F32)
        for bk in range(N_BUCKETS):
            acc = jnp.where(bucket == bk, relb_ref[bk, h], acc)
        bias_ref[h * CHUNK:(h + 1) * CHUNK, :] = jnp.where(in_window, acc, MASK_VALUE)
        wcat_ref[:, h * CHUNK:(h + 1) * CHUNK] = jnp.where(tril, sguw_ref[h], 0.0).astype(BF16)


def _local_mix_pieces(seq_start, t0, sink_ref,
                      pa_ref, halo_ref, pc_ref, kh_ref, vh_ref,
                      sgub_ref, avg_ref, poolw_ref, pscale_ref, gain_ref,
                      o_ref, bias_ref, wcat_ref):
    rows = pa_ref.shape[0]
    gw = GROUP_WIDTH
    n_heads = gw // HEAD_DIM
    n_chunks = rows // CHUNK
    lane = lax.broadcasted_iota(jnp.int32, (1, gw), 1)
    gain = gain_ref[...]

    p = pa_ref[:, 2 * gw:3 * gw]
    halo = jnp.where(seq_start, 0.0, halo_ref[...])
    ext = jnp.concatenate([halo, p], axis=0)
    sums = []
    run = ext
    for shift in (1, 2, 4, 8):
        run = run + pltpu.roll(run, shift, 0)
        sums.append(run[POOL_HALO:])
    group = lane // (gw // len(POOL_WINDOWS))
    wsum = jnp.where(group == 0, sums[0],
                     jnp.where(group == 1, sums[1],
                               jnp.where(group == 2, sums[2], sums[3])))
    win = jnp.where(group == 0, POOL_WINDOWS[0],
                    jnp.where(group == 1, POOL_WINDOWS[1],
                              jnp.where(group == 2, POOL_WINDOWS[2], POOL_WINDOWS[3])))
    t_seq = t0 + lax.broadcasted_iota(jnp.int32, (rows, 1), 0)
    count = jnp.minimum(t_seq + 1, win).astype(F32)
    yb = wsum / count - p
    yb = _dot(yb.astype(BF16), poolw_ref[...]) * pscale_ref[...]
    o_ref[:, gw:2 * gw] = _rms(yb, gain[:, gw:2 * gw]).astype(o_ref.dtype)
    yield

    u = _gelu_tanh(pa_ref[:, 0:gw])
    v = _gelu_tanh(pa_ref[:, gw:2 * gw])
    v_hi, v_lo = _split_bf16(v)
    vc = v - _dot(jnp.concatenate([v_hi, v_lo], axis=1), avg_ref[...])
    sq_hi, sq_lo = _split_bf16(vc * vc)
    var = _dot(jnp.concatenate([sq_hi, sq_lo], axis=1), avg_ref[...])
    vn = (vc * lax.rsqrt(var + EPS)).astype(BF16)
    rhs = jnp.concatenate(
        [_mask_heads(vn[c * CHUNK:(c + 1) * CHUNK], n_heads) for c in range(n_chunks)], axis=1)
    mix = _dot(wcat_ref[...], rhs)
    for c in range(n_chunks):
        r0 = c * CHUNK
        ya = u[r0:r0 + CHUNK] * (mix[:, c * gw:(c + 1) * gw] + sgub_ref[...])
        o_ref[r0:r0 + CHUNK, 0:gw] = _rms(ya, gain[:, 0:gw]).astype(o_ref.dtype)
    yield

    ki = lax.broadcasted_iota(jnp.int32, (1, 2 * CHUNK), 1)
    head_of_row = lax.broadcasted_iota(jnp.int32, (n_heads * CHUNK, 1), 0) // CHUNK
    sink = jnp.zeros((n_heads * CHUNK, 1), F32)
    for h in range(n_heads):
        sink = jnp.where(head_of_row == h, sink_ref[h], sink)
    for c in range(n_chunks):
        r0 = c * CHUNK
        if c == 0:
            k_prev, v_prev = kh_ref[...], vh_ref[...]
        else:
            k_prev = pc_ref[r0 - CHUNK:r0, gw:2 * gw]
            v_prev = pc_ref[r0 - CHUNK:r0, 2 * gw:3 * gw]
        k2 = jnp.concatenate([k_prev, pc_ref[r0:r0 + CHUNK, gw:2 * gw]], axis=0)
        v2 = jnp.concatenate([v_prev, pc_ref[r0:r0 + CHUNK, 2 * gw:3 * gw]], axis=0)
        qm = _mask_heads(pc_ref[r0:r0 + CHUNK, 0:gw], n_heads)
        logits = _dot_nt(qm, k2) + bias_ref[...]
        if c == 0:
            logits = jnp.where(jnp.logical_or(ki >= CHUNK, jnp.logical_not(seq_start)),
                               logits, MASK_VALUE)
        mx = jnp.maximum(jnp.max(logits, axis=-1, keepdims=True), sink)
        e = jnp.exp(logits - mx)
        denom = jnp.sum(e, axis=-1, keepdims=True) + jnp.exp(sink - mx)
        probs = (e * (1.0 / denom)).astype(BF16)
        pcat = jnp.concatenate([probs[h * CHUNK:(h + 1) * CHUNK] for h in range(n_heads)], axis=1)
        yc = _dot(pcat, _mask_heads(v2, n_heads))
        o_ref[r0:r0 + CHUNK, 2 * gw:3 * gw] = _rms(yc, gain[:, 2 * gw:3 * gw]).astype(o_ref.dtype)
        yield


def _softplus(z):
    neg_abs = lax.bitcast_convert_type(
        lax.bitcast_convert_type(z, jnp.uint32) | jnp.uint32(0x80000000), F32)
    return jnp.maximum(z, 0.0) + jnp.log(1.0 + jnp.exp(neg_abs))


def _sb_attn_kernel(q_ref, k_ref, v_ref, upper_ref, gain_ref, o_ref,
                    qm_ref, acc_ref, carry_ref, cmin_ref):
    m = pl.program_id(1)
    n_sub = SB_SUBBLOCKS
    blk = q_ref.shape[0] // n_sub
    gw = q_ref.shape[1]
    half = gw // 2
    n_heads = gw // HEAD_DIM
    rows = n_heads * blk
    lane = lax.broadcasted_iota(jnp.int32, (1, gw), 1)
    lane_h = lax.broadcasted_iota(jnp.int32, (1, half), 1)

    for s in range(n_sub):
        q = q_ref[s * blk:(s + 1) * blk, :]
        for h in range(n_heads):
            qm_ref[s, h * blk:(h + 1) * blk, :] = jnp.where(lane // HEAD_DIM == h, q,
                                                            jnp.zeros_like(q))
    causal = (lax.broadcasted_iota(jnp.int32, (blk, blk), 1)
              < lax.broadcasted_iota(jnp.int32, (blk, blk), 0))

    def keep_causal(x):
        return jnp.where(causal, x.reshape(n_heads, blk, blk), 0.0).reshape(rows, blk)

    def tile(s, t, carry, diag):
        start = pl.multiple_of((n_sub * m + s - t) * blk, blk)
        z = _dot_nt(qm_ref[s], k_ref[pl.ds(start, blk), :])
        sp = _softplus(z)
        log_sig = z - sp
        if diag:
            sp = keep_causal(sp)
        tail = _dot(sp.astype(BF16), upper_ref[...])
        if carry is not None:
            tail = tail + jnp.concatenate([carry, carry], axis=1)
        w = jnp.exp(log_sig - tail)
        if diag:
            w = keep_causal(w)
        w = w.astype(BF16)
        rowsum = jnp.broadcast_to(jnp.sum(sp, axis=-1, keepdims=True), (rows, half))
        pv = []
        for p in range(n_heads // 2):
            r = _dot(w[2 * p * blk:(2 * p + 2) * blk],
                     v_ref[pl.ds(start, blk), p * half:(p + 1) * half])
            pv.append(jnp.where(lane_h < HEAD_DIM, r[:blk], r[blk:]))
        return pv, rowsum if carry is None else carry + rowsum

    def first_tiles(first_step):
        for s in range(n_sub):
            pv, carry = tile(s, 0, None, True)
            if s > 0 or not first_step:
                pv_b, carry = tile(s, 1, carry, False)
                pv = [a + b for a, b in zip(pv, pv_b)]
            carry_ref[s] = carry
            for p in range(n_heads // 2):
                acc_ref[s, p] = pv[p]
            cmin_ref[s] = jnp.min(carry)

    pl.when(m == 0)(functools.partial(first_tiles, True))
    pl.when(m > 0)(functools.partial(first_tiles, False))

    for s in range(n_sub):
        def more(state, s=s):
            t, cmin = state
            return jnp.logical_and(t <= n_sub * m + s, cmin <= EXP_UNDERFLOW)

        def visit(state, s=s):
            t, _ = state
            pv, carry = tile(s, t, carry_ref[s], False)
            carry_ref[s] = carry
            for p in range(n_heads // 2):
                acc_ref[s, p] += pv[p]
            return t + 1, jnp.min(carry)

        lax.while_loop(more, visit, (2, cmin_ref[s]))

    for s in range(n_sub):
        y = jnp.concatenate([acc_ref[s, p] for p in range(n_heads // 2)], axis=-1)
        o_ref[s * blk:(s + 1) * blk, :] = _rms(y, gain_ref[...]).astype(o_ref.dtype)


def _sb_attn(pd, upper, gain, batch, seq):
    m = pd.shape[0]
    gw = GROUP_WIDTH
    blk = SB_BLOCK
    n_sub = SB_SUBBLOCKS
    nq = seq // (n_sub * blk)
    n_heads = gw // HEAD_DIM
    whole = pl.BlockSpec(memory_space=pltpu.VMEM)
    return pl.pallas_call(
        _sb_attn_kernel,
        grid=(batch, nq),
        in_specs=[
            pl.BlockSpec((n_sub * blk, gw), lambda b, j: (b * nq + j, 0)),
            pl.BlockSpec((seq, gw), lambda b, j: (b, 1)),
            pl.BlockSpec((seq, gw), lambda b, j: (b, 2)),
            whole, whole,
        ],
        out_specs=pl.BlockSpec((n_sub * blk, gw), lambda b, j: (b * nq + j, 0)),
        out_shape=jax.ShapeDtypeStruct((m, gw), BF16),
        scratch_shapes=[pltpu.VMEM((n_sub, n_heads * blk, gw), BF16),
                        pltpu.VMEM((n_sub, n_heads // 2, blk, gw // 2), F32),
                        pltpu.VMEM((n_sub, n_heads * blk, gw // 2), F32),
                        pltpu.SMEM((n_sub,), F32)],
        compiler_params=pltpu.CompilerParams(
            dimension_semantics=("arbitrary", "arbitrary"), vmem_limit_bytes=VMEM_LIMIT_MIX),
        name="sb_attn",
    )(pd, pd, pd, upper, gain)


def _out_ffn_kernel(x_ref, yabc_ref, yd_ref, wout_ref, g_ref, wgu_ref, wdown_ref, gfin_ref,
                    o_ref, x2_ref, act_ref, *, final_norm):
    n_abc = yabc_ref.shape[1]
    d_ff = wdown_ref.shape[0]
    x2_ref[...] = (x_ref[...]
                   + _dot(yabc_ref[...], wout_ref[0:n_abc, :])
                   + _dot(yd_ref[...], wout_ref[n_abc:, :]))
    h = _rms(x2_ref[...], g_ref[...]).astype(BF16)
    for c in range(d_ff // FF_CHUNK):
        c0 = c * FF_CHUNK
        gate = _dot(h, wgu_ref[:, c0:c0 + FF_CHUNK])
        up = _dot(h, wgu_ref[:, d_ff + c0:d_ff + c0 + FF_CHUNK])
        act_ref[:, c0:c0 + FF_CHUNK] = (gate * (1.0 / (1.0 + jnp.exp(-gate))) * up).astype(BF16)
    out = x2_ref[...] + _dot(act_ref[...], wdown_ref[...])
    if final_norm:
        out = _rms(out, gfin_ref[...])
    o_ref[...] = out


def _out_ffn(x2d, yabc, yd, w_out, g, w_gu, w_down, g_final, layer, final_norm):
    m, d = x2d.shape
    tm = FFN_ROWS
    vec = pl.BlockSpec((1, d), lambda i: (0, 0))
    return pl.pallas_call(
        functools.partial(_out_ffn_kernel, final_norm=final_norm),
        grid=(m // tm,),
        in_specs=[
            pl.BlockSpec((tm, d), lambda i: (i, 0)),
            pl.BlockSpec((tm, yabc.shape[1]), lambda i: (i, 0)),
            pl.BlockSpec((tm, yd.shape[1]), lambda i: (i, 0)),
            _layer_weight(w_out, layer), vec,
            _layer_weight(w_gu, layer), _layer_weight(w_down, layer), vec,
        ],
        out_specs=pl.BlockSpec((tm, d), lambda i: (i, 0)),
        out_shape=jax.ShapeDtypeStruct((m, d), F32),
        scratch_shapes=[pltpu.VMEM((tm, d), F32), pltpu.VMEM((tm, w_down.shape[1]), BF16)],
        compiler_params=pltpu.CompilerParams(
            dimension_semantics=("arbitrary",), vmem_limit_bytes=VMEM_LIMIT_DENSE),
        name="out_ffn",
    )(x2d, yabc, yd, w_out, g, w_gu, w_down, g_final)


def _t5_bucket(dist):
    max_exact = N_BUCKETS // 2
    df = jnp.maximum(dist, 1).astype(F32)
    large = max_exact + (jnp.log(df / max_exact) / math.log(MAX_DISTANCE / max_exact)
                         * (N_BUCKETS - max_exact)).astype(jnp.int32)
    large = jnp.minimum(large, N_BUCKETS - 1)
    return jnp.where(dist < max_exact, dist, large)


def kernel(x, w_in, w_out, sgu_w, sgu_b, pool_w, pool_scale, swa_sinks, rel_bias, mix_out_gain,
           norm_mix, norm_ffn, w_gate_up, w_down, norm_final):
    batch, seq, d = x.shape
    depth = w_in.shape[0]
    gw = GROUP_WIDTH
    n_pool = len(POOL_WINDOWS)
    rows = batch * seq
    assert seq % (SB_BLOCK * SB_SUBBLOCKS) == 0
    assert rows % PROJ_ROWS == 0 and rows % FFN_ROWS == 0

    head_id = np.arange(gw) // HEAD_DIM
    avg = (head_id[:, None] == head_id[None, :]) / HEAD_DIM
    avg2 = jnp.asarray(np.concatenate([avg, avg], axis=0), BF16)
    upper = jnp.asarray(np.arange(SB_BLOCK)[:, None] > np.arange(SB_BLOCK)[None, :], BF16)
    dist = (jnp.arange(CHUNK)[:, None] + CHUNK) - jnp.arange(2 * CHUNK)[None, :]
    bucket = _t5_bucket(jnp.clip(dist, 0, CHUNK - 1)).astype(jnp.int32)

    w_in_b, w_out_b = w_in.astype(BF16), w_out.astype(BF16)
    w_gu_b, w_down_b = w_gate_up.astype(BF16), w_down.astype(BF16)
    pool_bd = (pool_w[:, :, :, None, :] * jnp.eye(n_pool, dtype=F32)[None, :, None, :, None])
    pool_bd = pool_bd.reshape(depth, gw, gw).astype(BF16)
    sgu_b_tile = jnp.repeat(jnp.swapaxes(sgu_b, 1, 2), HEAD_DIM, axis=2)
    gains = mix_out_gain.reshape(depth, 1, 4 * gw)

    x2d = x.reshape(rows, d)
    for l in range(depth):
        yabc, pd = _proj_mix(x2d, norm_mix[l].reshape(1, d), w_in_b, l, rel_bias, swa_sinks[l],
                             sgu_w[l], sgu_b_tile[l], avg2, pool_bd[l], pool_scale[l].reshape(1, gw),
                             bucket, gains[l, :, :3 * gw], seq)
        yd = _sb_attn(pd, upper, gains[l, :, 3 * gw:], batch, seq)
        x2d = _out_ffn(x2d, yabc, yd, w_out_b, norm_ffn[l].reshape(1, d), w_gu_b, w_down_b,
                       norm_final.reshape(1, d), l, l == depth - 1)
    return x2d.reshape(batch, seq, d)
```
